```python
import math
import jax
import jax.numpy as jnp
from jax import lax
import numpy as np

D_MODEL = 2048
BATCH = 8
SEQ = 2048
DEPTH = 2
DEC_BATCH = 32
DEC_SEQ = 4
PAST_LEN = 8192
PAGE_SIZE = 128

N_EVEN = (DEPTH + 1) // 2
N_ODD = DEPTH // 2
H_A = D_MODEL // 256
D_HA = 64
DV_A = 2 * D_HA
W_A = H_A * DV_A
H_B = 4
W_B = D_MODEL // 2
DK_B = W_B // H_B
CONV_W = 4
MLSTM_CHUNK = 128
H_C = D_MODEL // 128
D_HC = D_MODEL // H_C
QBLK = 128
N_BUCKETS = 32
MAX_DIST = 128
PEER_HEADS = 8
PEER_NK = 128
PEER_EXPERTS = PEER_NK * PEER_NK
PEER_TOPK = 16
PEER_DKEY = 256
PEER_CHUNK = 128
ALPHA = (2.0 * DEPTH) ** 0.25
BETA = (8.0 * DEPTH) ** -0.25
IN_SIZES = [H_A * 2 * D_HA, H_A * 2 * D_HA, W_A, W_B, W_B, W_B, H_B, H_B]
IN_EVEN = sum(IN_SIZES)
LN_EPS = 1e-5

kernel_name = "hybrid_diffattn_mlstm_stickbreak_peer_step"


def lambda_init(layer):
    return 0.8 - 0.6 * math.exp(-0.3 * layer)


def layer_norm(x, g, b):
    xf = x.astype(jnp.float32)
    mu = jnp.mean(xf, -1, keepdims=True)
    var = jnp.mean(jnp.square(xf - mu), -1, keepdims=True)
    return ((xf - mu) * lax.rsqrt(var + LN_EPS) * g + b).astype(x.dtype)


def head_rms_norm(x, g):
    xf = x.astype(jnp.float32)
    return xf * lax.rsqrt(jnp.mean(jnp.square(xf), -1, keepdims=True) + LN_EPS) * g


def head_layer_norm(x, g):
    xf = x.astype(jnp.float32)
    mu = jnp.mean(xf, -1, keepdims=True)
    var = jnp.mean(jnp.square(xf - mu), -1, keepdims=True)
    return (xf - mu) * lax.rsqrt(var + LN_EPS) * g


def t5_bucket(dist):
    n = jnp.maximum(dist, 0)
    exact = N_BUCKETS // 2
    large = exact + (jnp.log(jnp.maximum(n, exact).astype(jnp.float32) / exact)
                     / math.log(MAX_DIST / exact) * (N_BUCKETS - exact)).astype(jnp.int32)
    return jnp.where(n < exact, n, jnp.minimum(large, N_BUCKETS - 1))


def sweep_queries(fn, q, qpos):
    B, T = q.shape[0], q.shape[1]
    if T <= QBLK or T % QBLK != 0:
        return fn(q, qpos)
    nb = T // QBLK
    qb = jnp.moveaxis(q.reshape((B, nb, QBLK) + q.shape[2:]), 1, 0)
    out = lax.map(lambda a: fn(a[0], a[1]), (qb, qpos.reshape(nb, QBLK)))
    return jnp.moveaxis(out, 0, 1).reshape((B, T) + out.shape[3:])


def diff_attention(q, k_past, k_new, v_past, v_new, qpos, kpos, lam, rel_bias):
    P = k_past.shape[1]
    scale = D_HA ** -0.5

    def block(qb, pb):
        s = jnp.concatenate([jnp.einsum('bqhmd,bkhmd->bmhqk', qb, k_past),
                             jnp.einsum('bqhmd,bkhmd->bmhqk', qb, k_new)], -1).astype(jnp.float32) * scale
        bias = rel_bias[t5_bucket(pb[:, None] - kpos[None, :])].astype(jnp.float32)
        s = s + jnp.transpose(bias, (2, 0, 1))
        s = jnp.where(kpos[None, :] <= pb[:, None], s, -jnp.inf)
        prob = jax.nn.softmax(s, axis=-1)
        w = (prob[:, 0] - lam * prob[:, 1]).astype(v_new.dtype)
        return (jnp.einsum('bhqk,bkhe->bqhe', w[..., :P], v_past)
                + jnp.einsum('bhqk,bkhe->bqhe', w[..., P:], v_new))

    return sweep_queries(block, q, qpos)


def stick_breaking(q, k_past, k_new, v_past, v_new, qpos, kpos):
    P = k_past.shape[1]
    scale = D_HC ** -0.5

    def block(qb, pb):
        z = jnp.concatenate([jnp.einsum('bqhd,bkhd->bhqk', qb, k_past),
                             jnp.einsum('bqhd,bkhd->bhqk', qb, k_new)], -1).astype(jnp.float32) * scale
        mask = kpos[None, :] < pb[:, None]
        log_keep = jnp.where(mask, jax.nn.log_sigmoid(-z), 0.0)
        after = lax.cumsum(log_keep, axis=3, reverse=True) - log_keep
        a = jnp.where(mask, jnp.exp(jax.nn.log_sigmoid(z) + after), 0.0).astype(v_new.dtype)
        return (jnp.einsum('bhqk,bkhd->bqhd', a[..., :P], v_past)
                + jnp.einsum('bhqk,bkhd->bqhd', a[..., P:], v_new))

    return sweep_queries(block, q, qpos)


def mlstm_scan(q, k, v, li, lf, C0, n0, m0):
    B, S, H, d = q.shape
    L = MLSTM_CHUNK if S % MLSTM_CHUNK == 0 else S
    nc = S // L

    def to_chunks(a):
        a = a.reshape((B, nc, L) + a.shape[2:])
        return jnp.moveaxis(jnp.moveaxis(a, 1, 0), 3, 2)

    causal = jnp.tril(jnp.ones((L, L), bool))

    def step(carry, inp):
        C, n, m = carry
        qc, kc, vc, ic, fc = inp
        b = jnp.cumsum(fc, axis=-1)
        D = jnp.where(causal, b[..., :, None] - b[..., None, :] + ic[..., None, :], -jnp.inf)
        inter = b + m[..., None]
        mt = jnp.maximum(inter, jnp.max(D, -1))
        w = jnp.exp(D - mt[..., None]) * jnp.einsum('bhtd,bhsd->bhts', qc, kc)
        sc = jnp.exp(inter - mt)
        num = jnp.einsum('bhts,bhsd->bhtd', w, vc) + sc[..., None] * jnp.einsum('bhtk,bhkv->bhtv', qc, C)
        den = jnp.sum(w, -1) + sc * jnp.einsum('bhtk,bhk->bht', qc, n)
        h = num / jnp.maximum(jnp.abs(den), jnp.exp(-mt))[..., None]
        m_new = mt[..., -1]
        wl = jnp.exp(b[..., -1:] - b + ic - m_new[..., None])
        dec = jnp.exp(b[..., -1] + m - m_new)
        C_new = dec[..., None, None] * C + jnp.einsum('bhs,bhsk,bhsv->bhkv', wl, kc, vc)
        n_new = dec[..., None] * n + jnp.einsum('bhs,bhsk->bhk', wl, kc)
        return (C_new, n_new, m_new), h

    (C, n, m), h = lax.scan(step, (C0, n0, m0), (to_chunks(q), to_chunks(k), to_chunks(v), to_chunks(li), to_chunks(lf)))
    h = jnp.swapaxes(jnp.moveaxis(h, 0, 1), 2, 3).reshape(B, S, H, d)
    return h, C, n, m


def peer(x, wq, subkeys, u, v):
    B, S, D = x.shape
    T = B * S
    pad = (-T) % PEER_CHUNK
    xt = jnp.pad(x.reshape(T, D), ((0, pad), (0, 0)))

    def chunk(xb):
        qh = (xb @ wq).reshape(-1, PEER_HEADS, 2, PEER_DKEY // 2)
        s = jnp.einsum('thpd,hpnd->thpn', qh, subkeys).astype(jnp.float32)
        s1, i1 = lax.top_k(s[:, :, 0], PEER_TOPK)
        s2, i2 = lax.top_k(s[:, :, 1], PEER_TOPK)
        cand = (s1[..., :, None] + s2[..., None, :]).reshape(s1.shape[:2] + (PEER_TOPK * PEER_TOPK,))
        sc, ci = lax.top_k(cand, PEER_TOPK)
        idx = (jnp.take_along_axis(i1, ci // PEER_TOPK, -1) * PEER_NK
               + jnp.take_along_axis(i2, ci % PEER_TOPK, -1))
        g = jax.nn.softmax(sc, axis=-1)
        act = jax.nn.gelu(jnp.einsum('thkd,td->thk', u[idx], xb).astype(jnp.float32), approximate=False)
        return jnp.einsum('thk,thkd->td', (g * act).astype(v.dtype), v[idx])

    out = lax.map(chunk, xt.reshape(-1, PEER_CHUNK, D))
    return out.reshape(-1, D)[:T].reshape(B, S, D).astype(x.dtype)


def even_mixer(x, qpos, kpos, k_past, v_past, C0, n0, m0, conv0, p, i, lam_init):
    B, S, _ = x.shape
    proj = x @ p['w_in_even'][i]
    qa, ka, va, u, vb, ob, ig, fg = jnp.split(proj, [int(c) for c in np.cumsum(IN_SIZES)[:-1]], axis=-1)
    qa = qa.reshape(B, S, H_A, 2, D_HA)
    ka = ka.reshape(B, S, H_A, 2, D_HA)
    va = va.reshape(B, S, H_A, DV_A)
    lam = (jnp.exp(jnp.sum(p['lam_q1'][i] * p['lam_k1'][i])) - jnp.exp(jnp.sum(p['lam_q2'][i] * p['lam_k2'][i]))
           + lam_init).astype(jnp.float32)
    ya = diff_attention(qa, k_past.astype(ka.dtype), ka, v_past.astype(va.dtype), va, qpos, kpos, lam, p['rel_bias'])
    ya = (head_rms_norm(ya, p['diff_norm_g'][i]) * (1.0 - lam_init)).reshape(B, S, W_A).astype(x.dtype)
    u_ext = jnp.concatenate([conv0.astype(u.dtype), u], 1)
    cw = p['conv_w'][i]
    uc = p['conv_b'][i]
    for j in range(CONV_W):
        uc = uc + u_ext[:, j:j + S] * cw[j]
    ua = jax.nn.silu(uc)
    ua_h = ua.reshape(B, S, H_B, DK_B)
    qb = jnp.einsum('bshd,hde->bshe', ua_h, p['w_q_mlstm'][i])
    kb = jnp.einsum('bshd,hde->bshe', ua_h, p['w_k_mlstm'][i]) * DK_B ** -0.5
    vbh = vb.reshape(B, S, H_B, DK_B)
    li = ig.astype(jnp.float32) + p['b_ig'][i]
    lf = jax.nn.log_sigmoid(fg.astype(jnp.float32) + p['b_fg'][i])
    h, C, n, m = mlstm_scan(qb.astype(jnp.float32), kb.astype(jnp.float32), vbh.astype(jnp.float32), li, lf,
                            C0.astype(jnp.float32), n0.astype(jnp.float32), m0.astype(jnp.float32))
    hn = head_layer_norm(h, p['mlstm_norm_g'][i].reshape(H_B, DK_B)).reshape(B, S, W_B).astype(x.dtype)
    yb = jax.nn.sigmoid(ob) * (hn + p['mlstm_skip'][i] * ua)
    y = jnp.concatenate([ya, yb], -1) @ p['w_out_even'][i]
    return y, (ka, va), (C, n, m, u_ext[:, -(CONV_W - 1):])


def odd_mixer(x, qpos, kpos, k_past, v_past, p, i):
    B, S, _ = x.shape
    qkv = (x @ p['w_qkv_odd'][i]).reshape(B, S, 3, H_C, D_HC)
    q, k, v = qkv[:, :, 0], qkv[:, :, 1], qkv[:, :, 2]
    y = stick_breaking(q, k_past.astype(k.dtype), k, v_past.astype(v.dtype), v, qpos, kpos)
    return y.reshape(B, S, D_MODEL) @ p['w_out_odd'][i], (k, v)


def trunk(x, past_len, diff_past, mlstm_past, sb_past, p):
    S = x.shape[1]
    qpos = past_len + jnp.arange(S, dtype=jnp.int32)
    kpos = jnp.arange(past_len + S, dtype=jnp.int32)
    new_diff, new_mlstm, new_sb = [], [], []
    for layer in range(DEPTH):
        i = layer // 2
        if layer % 2 == 0:
            kp, vp = diff_past[i]
            C0, n0, m0, cv0 = mlstm_past[i]
            y, st_a, st_b = even_mixer(x, qpos, kpos, kp, vp, C0, n0, m0, cv0, p, i, lambda_init(layer))
            new_diff.append(st_a)
            new_mlstm.append(st_b)
        else:
            kp, vp = sb_past[i]
            y, st_c = odd_mixer(x, qpos, kpos, kp, vp, p, i)
            new_sb.append(st_c)
        x = layer_norm(ALPHA * x + y, p['ln_g'][layer, 0], p['ln_b'][layer, 0])
        f = peer(x, p['peer_wq'][layer], p['peer_subkeys'][layer], p['peer_u'][layer], p['peer_v'][layer])
        x = layer_norm(ALPHA * x + f, p['ln_g'][layer, 1], p['ln_b'][layer, 1])
    return x, new_diff, new_mlstm, new_sb


def setup_inputs(seed: int = 0) -> dict:
    key = jax.random.key(seed)
    ks = iter(jax.random.split(key, 48))

    def nrm(shape, scale=1.0):
        return jax.random.normal(next(ks), shape, jnp.float32) * scale

    n_pages = PAST_LEN // PAGE_SIZE
    n_used = DEC_BATCH * n_pages
    n_pool = n_used + max(1, n_used // 4)
    page_table = jax.random.permutation(next(ks), n_pool)[:n_used].reshape(DEC_BATCH, n_pages).astype(jnp.int32)
    d_in = D_MODEL ** -0.5
    return {
        'x_prompt': nrm((BATCH, SEQ, D_MODEL)),
        'x_sample': nrm((DEC_BATCH, DEC_SEQ, D_MODEL)),
        'cache_diff_k': nrm((N_EVEN, n_pool, PAGE_SIZE, H_A, 2, D_HA)),
        'cache_diff_v': nrm((N_EVEN, n_pool, PAGE_SIZE, H_A, DV_A)),
        'state_mlstm_C': nrm((N_EVEN, DEC_BATCH, H_B, DK_B, DK_B), 0.1),
        'state_mlstm_n': nrm((N_EVEN, DEC_BATCH, H_B, DK_B)),
        'state_mlstm_m': nrm((N_EVEN, DEC_BATCH, H_B)),
        'state_mlstm_conv': nrm((N_EVEN, DEC_BATCH, CONV_W - 1, W_B)),
        'cache_sb_k': nrm((N_ODD, n_pool, PAGE_SIZE, H_C, D_HC)),
        'cache_sb_v': nrm((N_ODD, n_pool, PAGE_SIZE, H_C, D_HC)),
        'page_table': page_table,
        'rel_bias': nrm((N_BUCKETS, H_A), 0.5),
        'w_in_even': nrm((N_EVEN, D_MODEL, IN_EVEN), d_in),
        'b_ig': nrm((N_EVEN, H_B), 0.1),
        'b_fg': jnp.linspace(3.0, 6.0, H_B)[None, :] + nrm((N_EVEN, H_B), 0.1),
        'lam_q1': nrm((N_EVEN, D_HA), 0.1),
        'lam_k1': nrm((N_EVEN, D_HA), 0.1),
        'lam_q2': nrm((N_EVEN, D_HA), 0.1),
        'lam_k2': nrm((N_EVEN, D_HA), 0.1),
        'diff_norm_g': 1.0 + nrm((N_EVEN, DV_A), 0.02),
        'conv_w': nrm((N_EVEN, CONV_W, W_B), CONV_W ** -0.5),
        'conv_b': nrm((N_EVEN, W_B), 0.02),
        'w_q_mlstm': nrm((N_EVEN, H_B, DK_B, DK_B), DK_B ** -0.5),
        'w_k_mlstm': nrm((N_EVEN, H_B, DK_B, DK_B), DK_B ** -0.5),
        'mlstm_norm_g': 1.0 + nrm((N_EVEN, W_B), 0.02),
        'mlstm_skip': 1.0 + nrm((N_EVEN, W_B), 0.02),
        'w_out_even': nrm((N_EVEN, D_MODEL, D_MODEL), d_in * BETA),
        'w_qkv_odd': nrm((N_ODD, D_MODEL, 3 * D_MODEL), d_in),
        'w_out_odd': nrm((N_ODD, D_MODEL, D_MODEL), d_in * BETA),
        'ln_g': 1.0 + nrm((DEPTH, 2, D_MODEL), 0.02),
        'ln_b': nrm((DEPTH, 2, D_MODEL), 0.02),
        'peer_wq': nrm((DEPTH, D_MODEL, PEER_HEADS * PEER_DKEY), d_in),
        'peer_subkeys': nrm((DEPTH, PEER_HEADS, 2, PEER_NK, PEER_DKEY // 2), (PEER_DKEY // 2) ** -0.5),
        'peer_u': nrm((DEPTH, PEER_EXPERTS, D_MODEL), d_in),
        'peer_v': nrm((DEPTH, PEER_EXPERTS, D_MODEL), BETA * PEER_HEADS ** -0.5),
    }


def reference(x_prompt, x_sample, cache_diff_k, cache_diff_v, state_mlstm_C, state_mlstm_n, state_mlstm_m,
              state_mlstm_conv, cache_sb_k, cache_sb_v, page_table, rel_bias, w_in_even, b_ig, b_fg,
              lam_q1, lam_k1, lam_q2, lam_k2, diff_norm_g, conv_w, conv_b, w_q_mlstm, w_k_mlstm,
              mlstm_norm_g, mlstm_skip, w_out_even, w_qkv_odd, w_out_odd, ln_g, ln_b,
              peer_wq, peer_subkeys, peer_u, peer_v):
    p = {'rel_bias': rel_bias, 'w_in_even': w_in_even, 'b_ig': b_ig, 'b_fg': b_fg,
         'lam_q1': lam_q1, 'lam_k1': lam_k1, 'lam_q2': lam_q2, 'lam_k2': lam_k2,
         'diff_norm_g': diff_norm_g, 'conv_w': conv_w, 'conv_b': conv_b,
         'w_q_mlstm': w_q_mlstm, 'w_k_mlstm': w_k_mlstm, 'mlstm_norm_g': mlstm_norm_g,
         'mlstm_skip': mlstm_skip, 'w_out_even': w_out_even, 'w_qkv_odd': w_qkv_odd,
         'w_out_odd': w_out_odd, 'ln_g': ln_g, 'ln_b': ln_b, 'peer_wq': peer_wq,
         'peer_subkeys': peer_subkeys, 'peer_u': peer_u, 'peer_v': peer_v}
    B = x_prompt.shape[0]
    DB = x_sample.shape[0]
    past_len = page_table.shape[1] * PAGE_SIZE
    dt = x_prompt.dtype
    f32 = jnp.float32

    def gather(pool):
        return pool[page_table].reshape((DB, past_len) + pool.shape[2:])

    diff_p = [(jnp.zeros((B, 0, H_A, 2, D_HA), dt), jnp.zeros((B, 0, H_A, DV_A), dt)) for _ in range(N_EVEN)]
    mlstm_p = [(jnp.zeros((B, H_B, DK_B, DK_B), f32), jnp.zeros((B, H_B, DK_B), f32),
                jnp.zeros((B, H_B), f32), jnp.zeros((B, CONV_W - 1, W_B), dt)) for _ in range(N_EVEN)]
    sb_p = [(jnp.zeros((B, 0, H_C, D_HC), dt), jnp.zeros((B, 0, H_C, D_HC), dt)) for _ in range(N_ODD)]
    y_prompt, nd_p, nm_p, ns_p = trunk(x_prompt, 0, diff_p, mlstm_p, sb_p, p)

    diff_s = [(gather(cache_diff_k[i]), gather(cache_diff_v[i])) for i in range(N_EVEN)]
    mlstm_s = [(state_mlstm_C[i], state_mlstm_n[i], state_mlstm_m[i], state_mlstm_conv[i]) for i in range(N_EVEN)]
    sb_s = [(gather(cache_sb_k[i]), gather(cache_sb_v[i])) for i in range(N_ODD)]
    y_sample, nd_s, nm_s, ns_s = trunk(x_sample, past_len, diff_s, mlstm_s, sb_s, p)

    def stk(lst, j):
        return jnp.stack([e[j] for e in lst])

    return (y_prompt, y_sample,
            stk(nd_p, 0), stk(nd_p, 1), stk(nd_s, 0), stk(nd_s, 1),
            stk(nm_p, 0), stk(nm_p, 1), stk(nm_p, 2), stk(nm_p, 3),
            stk(nm_s, 0), stk(nm_s, 1), stk(nm_s, 2), stk(nm_s, 3),
            stk(ns_p, 0), stk(ns_p, 1), stk(ns_s, 0), stk(ns_s, 1))
```

```python
import functools
import math

import jax
import jax.numpy as jnp
import numpy as np
from jax import lax
from jax.experimental import pallas as pl
from jax.experimental.pallas import tpu as pltpu

F32 = jnp.float32
BF16 = jnp.bfloat16

LN_EPS = 1e-5
NEG_BIG = -1e30
VMEM_LIMIT = 48 * 1024 * 1024

D_HA = 64
N_BUCKETS = 32
MAX_DIST = 128
CONV_W = 4
MLSTM_CHUNK = 128
PAGE = 128
PEER_HEADS = 8
PEER_NK = 128
PEER_TOPK = 16


def _cparams(*sem):
    return pltpu.CompilerParams(dimension_semantics=sem, vmem_limit_bytes=VMEM_LIMIT)


def _dot(a, b):
    return jnp.dot(a, b, preferred_element_type=F32)


def _dot_nt(a, b):
    return lax.dot_general(a, b, (((1,), (1,)), ((), ())), preferred_element_type=F32)


def _dot_tn(a, b):
    return lax.dot_general(a, b, (((0,), (0,)), ((), ())), preferred_element_type=F32)


def _split3(x):
    h1 = x.astype(BF16)
    r1 = x - h1.astype(F32)
    h2 = r1.astype(BF16)
    h3 = (r1 - h2.astype(F32)).astype(BF16)
    return h1, h2, h3


def _dot_exact_rhs(x, m_bf16):
    h1, h2, h3 = _split3(x)
    return _dot(h1, m_bf16) + _dot(h2, m_bf16) + _dot(h3, m_bf16)


def _row_tile(m):
    for t in (384, 256, 128, 64, 32, 16, 8):
        if m % t == 0:
            return t
    raise ValueError(f"row count {m} is not a multiple of 8")


def _mm_kernel(x_ref, w_ref, o_ref):
    o_ref[...] = _dot(x_ref[...].astype(BF16), w_ref[...])


def _matmul(x, w_bf16, tn):
    m, k = x.shape
    n = w_bf16.shape[1]
    tm = _row_tile(m)
    return pl.pallas_call(
        _mm_kernel,
        grid=(m // tm, n // tn),
        in_specs=[pl.BlockSpec((tm, k), lambda i, j: (i, 0)),
                  pl.BlockSpec((k, tn), lambda i, j: (0, j))],
        out_specs=pl.BlockSpec((tm, tn), lambda i, j: (i, j)),
        out_shape=jax.ShapeDtypeStruct((m, n), F32),
        compiler_params=_cparams("parallel", "arbitrary"),
        name="matmul",
    )(x, w_bf16)


def _mm_nt_kernel(w_ref, x_ref, o_ref):
    o_ref[...] = _dot_nt(w_ref[...], x_ref[...].astype(BF16))


def _matmul_t(wt_bf16, x):
    n, k = wt_bf16.shape
    m = x.shape[0]
    tm = _row_tile(m)
    return pl.pallas_call(
        _mm_nt_kernel,
        grid=(m // tm,),
        in_specs=[pl.BlockSpec((n, k), lambda i: (0, 0)),
                  pl.BlockSpec((tm, k), lambda i: (i, 0))],
        out_specs=pl.BlockSpec((n, tm), lambda i: (0, i)),
        out_shape=jax.ShapeDtypeStruct((n, m), F32),
        compiler_params=_cparams("parallel"),
        name="matmul_t",
    )(wt_bf16, x)


def _layer_norm(xf, g, b):
    mu = jnp.mean(xf, -1, keepdims=True)
    xc = xf - mu
    var = jnp.mean(xc * xc, -1, keepdims=True)
    return xc * lax.rsqrt(var + LN_EPS) * g + b


def _mm_res_ln_kernel(alpha, a_ref, w_ref, x_ref, g_ref, b_ref, o_ref):
    y = _dot(a_ref[...].astype(BF16), w_ref[...])
    o_ref[...] = _layer_norm(alpha * x_ref[...] + y, g_ref[...], b_ref[...])


def _matmul_res_ln(a, w_bf16, x, g, b, alpha):
    m, k = a.shape
    n = w_bf16.shape[1]
    tm = _row_tile(m)
    return pl.pallas_call(
        functools.partial(_mm_res_ln_kernel, alpha),
        grid=(m // tm,),
        in_specs=[pl.BlockSpec((tm, k), lambda i: (i, 0)),
                  pl.BlockSpec((k, n), lambda i: (0, 0)),
                  pl.BlockSpec((tm, n), lambda i: (i, 0)),
                  pl.BlockSpec((1, n), lambda i: (0, 0)),
                  pl.BlockSpec((1, n), lambda i: (0, 0))],
        out_specs=pl.BlockSpec((tm, n), lambda i: (i, 0)),
        out_shape=jax.ShapeDtypeStruct((m, n), F32),
        compiler_params=_cparams("parallel"),
        name="matmul_res_ln",
    )(a, w_bf16, x, g.reshape(1, n), b.reshape(1, n))


def _res_ln_kernel(alpha, x_ref, f_ref, g_ref, b_ref, o_ref):
    o_ref[...] = _layer_norm(alpha * x_ref[...] + f_ref[...], g_ref[...], b_ref[...])


def _res_ln(x, f, g, b, alpha):
    m, n = x.shape
    tm = _row_tile(m)
    return pl.pallas_call(
        functools.partial(_res_ln_kernel, alpha),
        grid=(m // tm,),
        in_specs=[pl.BlockSpec((tm, n), lambda i: (i, 0)),
                  pl.BlockSpec((tm, n), lambda i: (i, 0)),
                  pl.BlockSpec((1, n), lambda i: (0, 0)),
                  pl.BlockSpec((1, n), lambda i: (0, 0))],
        out_specs=pl.BlockSpec((tm, n), lambda i: (i, 0)),
        out_shape=jax.ShapeDtypeStruct((m, n), F32),
        compiler_params=_cparams("parallel"),
        name="res_ln",
    )(x, f, g.reshape(1, n), b.reshape(1, n))


def _t5_bucket(dist):
    n = jnp.maximum(dist, 0)
    exact = N_BUCKETS // 2
    large = exact + (jnp.log(jnp.maximum(n, exact).astype(F32) / exact)
                     / math.log(MAX_DIST / exact) * (N_BUCKETS - exact)).astype(jnp.int32)
    return jnp.where(n < exact, n, jnp.minimum(large, N_BUCKETS - 1))


def _lambda(lam_ref, lam_init):
    lq1, lk1, lq2, lk2 = lam_ref[0:1, :], lam_ref[1:2, :], lam_ref[2:3, :], lam_ref[3:4, :]
    return (jnp.exp(jnp.sum(lq1 * lk1, keepdims=True)) - jnp.exp(jnp.sum(lq2 * lk2, keepdims=True))
            + lam_init)


def _diff_finish(acc1, l1, acc2, l2, lam, g, lam_init):
    o = acc1 / l1 - lam * (acc2 / l2)
    ms = jnp.mean(o * o, -1, keepdims=True)
    return o * lax.rsqrt(ms + LN_EPS) * g * (1.0 - lam_init)


def _diff_prompt_kernel(tq, lam_init, q_ref, k_ref, v_ref, bias_ref, far_ref, lam_ref, g_ref, o_ref):
    qi = pl.program_id(2)
    scale = D_HA ** -0.5
    q = q_ref[...]
    lane = lax.broadcasted_iota(jnp.int32, q.shape, 1)
    q1 = jnp.where(lane < D_HA, q, 0.0).astype(BF16)
    q2 = jnp.where(lane >= D_HA, q, 0.0).astype(BF16)
    dv = v_ref.shape[-1]

    def update(state, ki, bias, causal):
        m1, l1, a1, m2, l2, a2 = state
        start = pl.multiple_of(ki * tq, tq)
        kb = k_ref[pl.ds(start, tq), :].astype(BF16)
        vb = v_ref[pl.ds(start, tq), :].astype(BF16)
        out = []
        for qm, m, l, a in ((q1, m1, l1, a1), (q2, m2, l2, a2)):
            s = _dot_nt(qm, kb) * scale + bias
            if causal:
                r = lax.broadcasted_iota(jnp.int32, s.shape, 0)
                c = lax.broadcasted_iota(jnp.int32, s.shape, 1)
                s = jnp.where(c <= r, s, NEG_BIG)
            mn = jnp.maximum(m, jnp.max(s, -1, keepdims=True))
            p = jnp.exp(s - mn)
            alpha = jnp.exp(m - mn)
            out += [mn, alpha * l + jnp.sum(p, -1, keepdims=True), alpha * a + _dot(p.astype(BF16), vb)]
        return tuple(out)

    init = (jnp.full((tq, 1), NEG_BIG, F32), jnp.zeros((tq, 1), F32), jnp.zeros((tq, dv), F32)) * 2
    state = update(init, qi, bias_ref[0], True)
    sub = jnp.where(qi >= 1, bias_ref[1], NEG_BIG)
    state = update(state, jnp.maximum(qi - 1, 0), sub, False)
    far = far_ref[...]
    m1, l1, a1, m2, l2, a2 = lax.fori_loop(0, jnp.maximum(qi - 1, 0),
                                           lambda ki, st: update(st, ki, far, False), state)
    o_ref[...] = _diff_finish(a1, l1, a2, l2, _lambda(lam_ref, lam_init), g_ref[...], lam_init)


def _diff_bias_tables(rel_bias, tq):
    h = rel_bias.shape[1]
    r = jnp.arange(tq, dtype=jnp.int32)[:, None]
    c = jnp.arange(tq, dtype=jnp.int32)[None, :]
    tiles = jnp.stack([rel_bias[_t5_bucket(r - c)], rel_bias[_t5_bucket(r - c + tq)]])
    return jnp.transpose(tiles, (3, 0, 1, 2)), rel_bias[N_BUCKETS - 1].reshape(h, 1, 1)


def _diff_attention_prompt(proj, batch, seq, n_head, rel_bias, lam_vec, g, lam_init, tq):
    assert tq >= MAX_DIST and seq % tq == 0
    dh = 2 * D_HA
    nq = seq // tq
    bias_tiles, far = _diff_bias_tables(rel_bias, tq)
    return pl.pallas_call(
        functools.partial(_diff_prompt_kernel, tq, lam_init),
        grid=(batch, n_head, nq),
        in_specs=[pl.BlockSpec((tq, dh), lambda b, h, i: (b * nq + i, h)),
                  pl.BlockSpec((seq, dh), lambda b, h, i: (b, n_head + h)),
                  pl.BlockSpec((seq, dh), lambda b, h, i: (b, 2 * n_head + h)),
                  pl.BlockSpec((None, 2, tq, tq), lambda b, h, i: (h, 0, 0, 0)),
                  pl.BlockSpec((None, 1, 1), lambda b, h, i: (h, 0, 0)),
                  pl.BlockSpec((4, D_HA), lambda b, h, i: (0, 0)),
                  pl.BlockSpec((1, dh), lambda b, h, i: (0, 0))],
        out_specs=pl.BlockSpec((tq, dh), lambda b, h, i: (b * nq + i, h)),
        out_shape=jax.ShapeDtypeStruct((batch * seq, n_head * dh), F32),
        compiler_params=_cparams("parallel", "parallel", "arbitrary"),
        name="diff_attn_prompt",
    )(proj, proj, proj, bias_tiles, far, lam_vec, g.reshape(1, dh))


def _log_sigmoid_pair(z):
    t = jnp.log(1.0 + jnp.exp(-jnp.abs(z)))
    return jnp.minimum(z, 0.0) - t, jnp.minimum(-z, 0.0) - t


def _suffix_matrix(n):
    j = lax.broadcasted_iota(jnp.int32, (n, n), 0)
    s = lax.broadcasted_iota(jnp.int32, (n, n), 1)
    return jnp.where(j > s, 1.0, 0.0).astype(BF16)


def _sb_prompt_kernel(tq, scale, q_ref, k_ref, v_ref, o_ref):
    qi = pl.program_id(2)
    q = q_ref[...].astype(BF16)
    dv = v_ref.shape[-1]
    suffix = _suffix_matrix(tq)

    def update(state, ki, causal):
        run, acc = state
        start = pl.multiple_of(ki * tq, tq)
        kb = k_ref[pl.ds(start, tq), :].astype(BF16)
        vb = v_ref[pl.ds(start, tq), :].astype(BF16)
        z = _dot_nt(q, kb) * scale
        ls, lk = _log_sigmoid_pair(z)
        if causal:
            r = lax.broadcasted_iota(jnp.int32, z.shape, 0)
            c = lax.broadcasted_iota(jnp.int32, z.shape, 1)
            mask = c < r
            lk = jnp.where(mask, lk, 0.0)
        after = _dot_exact_rhs(lk, suffix) + run
        a = jnp.exp(ls + after)
        if causal:
            a = jnp.where(mask, a, 0.0)
        return run + jnp.sum(lk, -1, keepdims=True), acc + _dot(a.astype(BF16), vb)

    state = update((jnp.zeros((tq, 1), F32), jnp.zeros((tq, dv), F32)), qi, True)
    _, acc = lax.fori_loop(0, qi, lambda j, st: update(st, qi - 1 - j, False), state)
    o_ref[...] = acc


def _sb_attention_prompt(qkv, batch, seq, n_head, dh, tq):
    nq = seq // tq
    return pl.pallas_call(
        functools.partial(_sb_prompt_kernel, tq, dh ** -0.5),
        grid=(batch, n_head, nq),
        in_specs=[pl.BlockSpec((tq, dh), lambda b, h, i: (b * nq + i, h)),
                  pl.BlockSpec((seq, dh), lambda b, h, i: (b, n_head + h)),
                  pl.BlockSpec((seq, dh), lambda b, h, i: (b, 2 * n_head + h))],
        out_specs=pl.BlockSpec((tq, dh), lambda b, h, i: (b * nq + i, h)),
        out_shape=jax.ShapeDtypeStruct((batch * seq, n_head * dh), F32),
        compiler_params=_cparams("parallel", "parallel", "arbitrary"),
        name="sb_attn_prompt",
    )(qkv, qkv, qkv)


def _mlstm_kernel(L, u_ref, v_ref, ob_ref, grow_ref, bcol_ref, conv0_ref, cw_ref, cb_ref,
                  wq_ref, wk_ref, c0_ref, n0_ref, m0_ref, g_ref, skip_ref,
                  y_ref, c_out, n_out, m_out, ext, c_s, n_s, m_s):
    c = pl.program_id(2)
    dk = u_ref.shape[-1]

    @pl.when(c == 0)
    def _():
        ext[0:8, :] = conv0_ref[...]
        c_s[...] = c0_ref[...]
        n_s[...] = n0_ref[...]
        m_s[...] = m0_ref[...]

    ext[8:8 + L, :] = u_ref[...]
    uc = cb_ref[...]
    for j in range(CONV_W):
        uc = uc + ext[pl.ds(8 - (CONV_W - 1) + j, L), :] * cw_ref[j:j + 1, :]
    ext[0:8, :] = ext[L:L + 8, :]
    ua = uc * jax.nn.sigmoid(uc)
    uab = ua.astype(BF16)
    qf = _dot(uab, wq_ref[...])
    q = qf.astype(BF16)
    kf = _dot(uab, wk_ref[...]) * dk ** -0.5
    k = kf.astype(BF16)
    vb = v_ref[...].astype(BF16)

    gr = grow_ref[...] + bcol_ref[...]
    li_r, lf_r = gr[0:1, :], _log_sigmoid_pair(gr[1:2, :])[0]
    t = lax.broadcasted_iota(jnp.int32, (L, L), 0)
    s = lax.broadcasted_iota(jnp.int32, (L, L), 1)
    causal = s <= t
    li_c = jnp.sum(jnp.where(s == t, li_r, 0.0), -1, keepdims=True)
    b_c = _dot_exact_rhs(jnp.where(causal, lf_r, 0.0), jnp.ones((L, 8), BF16))[:, 0:1]
    b_r = _dot_exact_rhs(jnp.broadcast_to(lf_r, (8, L)), jnp.where(t <= s, 1.0, 0.0).astype(BF16))[0:1, :]

    m = m_s[:, 0:1]
    cmat = c_s[...]
    nrow = n_s[...]
    d = jnp.where(causal, b_c - b_r + li_r, NEG_BIG)
    inter = b_c + m
    mt = jnp.maximum(inter, jnp.max(d, -1, keepdims=True))
    w = jnp.exp(d - mt) * _dot_nt(q, k)
    sc = jnp.exp(inter - mt)
    num = _dot(w.astype(BF16), vb) + sc * _dot(q, cmat.astype(BF16))
    den = jnp.sum(w, -1, keepdims=True) + sc * jnp.sum(qf * nrow, -1, keepdims=True)
    h = num / jnp.maximum(jnp.abs(den), jnp.exp(-mt))

    m_new = mt[L - 1:L, :]
    b_last = b_c[L - 1:L, :]
    wl = jnp.exp(b_last - b_c + li_c - m_new)
    dec = jnp.exp(b_last + m - m_new)
    kw = kf * wl
    c_s[...] = dec * cmat + _dot_tn(kw.astype(BF16), vb)
    n_s[...] = dec * nrow + jnp.sum(kw, 0, keepdims=True)
    m_s[...] = jnp.broadcast_to(m_new, m_s.shape)

    mu = jnp.mean(h, -1, keepdims=True)
    hc = h - mu
    var = jnp.mean(hc * hc, -1, keepdims=True)
    hn = hc * lax.rsqrt(var + LN_EPS) * g_ref[...]
    y_ref[...] = jax.nn.sigmoid(ob_ref[...]) * (hn + skip_ref[...] * ua)

    @pl.when(c == pl.num_programs(2) - 1)
    def _():
        c_out[...] = c_s[...]
        n_out[...] = n_s[...]
        m_out[...] = m_s[...]


def _mlstm(rows, gates, batch, seq, L, n_head, lane0, conv0, c0, n0, m0, p):
    dk = 256
    nc = seq // L
    t_rows = batch * seq
    grow = jnp.transpose(gates.reshape(batch * nc, L, 2, n_head), (3, 0, 2, 1))
    bias = jnp.stack([p['b_ig'], p['b_fg']], -1)
    outs = pl.pallas_call(
        functools.partial(_mlstm_kernel, L),
        grid=(batch, n_head, nc),
        in_specs=[pl.BlockSpec((L, dk), lambda b, h, c: (b * nc + c, lane0 + h)),
                  pl.BlockSpec((L, dk), lambda b, h, c: (b * nc + c, lane0 + n_head + h)),
                  pl.BlockSpec((L, dk), lambda b, h, c: (b * nc + c, lane0 + 2 * n_head + h)),
                  pl.BlockSpec((None, None, 2, L), lambda b, h, c: (h, b * nc + c, 0, 0)),
                  pl.BlockSpec((None, 2, 1), lambda b, h, c: (h, 0, 0)),
                  pl.BlockSpec((None, 8, dk), lambda b, h, c: (b, 0, h)),
                  pl.BlockSpec((CONV_W, dk), lambda b, h, c: (0, h)),
                  pl.BlockSpec((1, dk), lambda b, h, c: (0, h)),
                  pl.BlockSpec((None, dk, dk), lambda b, h, c: (h, 0, 0)),
                  pl.BlockSpec((None, dk, dk), lambda b, h, c: (h, 0, 0)),
                  pl.BlockSpec((None, None, dk, dk), lambda b, h, c: (b, h, 0, 0)),
                  pl.BlockSpec((None, None, 1, dk), lambda b, h, c: (b, h, 0, 0)),
                  pl.BlockSpec((None, None, 1, 128), lambda b, h, c: (b, h, 0, 0)),
                  pl.BlockSpec((1, dk), lambda b, h, c: (0, h)),
                  pl.BlockSpec((1, dk), lambda b, h, c: (0, h))],
        out_specs=[pl.BlockSpec((L, dk), lambda b, h, c: (b * nc + c, h)),
                   pl.BlockSpec((None, None, dk, dk), lambda b, h, c: (b, h, 0, 0)),
                   pl.BlockSpec((None, None, 1, dk), lambda b, h, c: (b, h, 0, 0)),
                   pl.BlockSpec((None, None, 1, 128), lambda b, h, c: (b, h, 0, 0))],
        out_shape=[jax.ShapeDtypeStruct((t_rows, n_head * dk), F32),
                   jax.ShapeDtypeStruct((batch, n_head, dk, dk), F32),
                   jax.ShapeDtypeStruct((batch, n_head, 1, dk), F32),
                   jax.ShapeDtypeStruct((batch, n_head, 1, 128), F32)],
        scratch_shapes=[pltpu.VMEM((L + 8, dk), F32), pltpu.VMEM((dk, dk), F32),
                        pltpu.VMEM((1, dk), F32), pltpu.VMEM((1, 128), F32)],
        compiler_params=_cparams("parallel", "parallel", "arbitrary"),
        name="mlstm",
    )(rows, rows, rows, grow, bias.reshape(n_head, 2, 1), conv0,
      p['conv_w'], p['conv_b'].reshape(1, -1), p['w_q_mlstm'].astype(BF16), p['w_k_mlstm'].astype(BF16),
      c0, n0.reshape(batch, n_head, 1, dk), jnp.broadcast_to(m0[:, :, None, None], (batch, n_head, 1, 128)),
      p['mlstm_norm_g'].reshape(1, -1), p['mlstm_skip'].reshape(1, -1))
    y, c_new, n_new, m_new = outs
    return y, c_new, n_new.reshape(batch, n_head, dk), m_new[:, :, 0, 0]


def _segment_matrix(n_seg, width, order):
    rows = np.zeros((n_seg, n_seg * width), np.float32)
    for i in range(n_seg):
        rows[order[i], i * width:(i + 1) * width] = 1.0
    return jnp.asarray(rows, BF16)


def _segment_scores(seg, kq):
    h1, h2, _ = _split3(kq)
    return _dot_nt(seg, h1) + _dot_nt(seg, h2)


def _head_block_sum(x, n_head, width):
    rows, lanes = x.shape
    r = lax.broadcasted_iota(jnp.int32, x.shape, 0)
    c = lax.broadcasted_iota(jnp.int32, x.shape, 1)
    own = (c // width) == (r % n_head)
    return jnp.sum(jnp.where(own, x, 0.0).reshape(rows // n_head, n_head, lanes), axis=1)


def _diff_sample_kernel(n_q, n_page, lam_init, pt_ref, q_ref, kc_ref, vc_ref, kn_ref, vn_ref, bias_ref, seg_ref,
                        lam_ref, g_ref, o_ref, m_s, l_s, a_s):
    p = pl.program_id(1)
    n_head = seg_ref.shape[0] // 2
    dv = vc_ref.shape[-1] // n_head
    scale = D_HA ** -0.5

    @pl.when(p == 0)
    def _():
        m_s[...] = jnp.full(m_s.shape, NEG_BIG, F32)
        l_s[...] = jnp.zeros(l_s.shape, F32)
        a_s[...] = jnp.zeros(a_s.shape, F32)

    def block(k_ref, v_ref):
        k = k_ref[...]
        sq = [_segment_scores(seg_ref[...], k * q_ref[i:i + 1, :]) for i in range(n_q)]
        vb = v_ref[...].astype(BF16)
        for mi in range(2):
            s = jnp.concatenate([x[mi * n_head:(mi + 1) * n_head] for x in sq], 0) * scale + bias_ref[...]
            m = m_s[mi]
            mn = jnp.maximum(m, jnp.max(s, -1, keepdims=True))
            pr = jnp.exp(s - mn)
            alpha = jnp.exp(m - mn)
            m_s[mi] = mn
            l_s[mi] = alpha * l_s[mi] + jnp.sum(pr, -1, keepdims=True)
            a_s[mi] = alpha * a_s[mi] + _dot(pr.astype(BF16), vb)

    @pl.when(p < n_page)
    def _():
        block(kc_ref, vc_ref)

    @pl.when(p == n_page)
    def _():
        block(kn_ref, vn_ref)
        o = []
        for mi in range(2):
            num = _head_block_sum(a_s[mi], n_head, dv)
            den = _head_block_sum(jnp.broadcast_to(l_s[mi], a_s.shape[1:]), n_head, dv)
            o.append(num / den)
        lam = _lambda(lam_ref, lam_init)
        od = o[0] - lam * o[1]
        for h in range(n_head):
            oh = od[:, h * dv:(h + 1) * dv]
            ms = jnp.mean(oh * oh, -1, keepdims=True)
            o_ref[:, h * dv:(h + 1) * dv] = oh * lax.rsqrt(ms + LN_EPS) * g_ref[...] * (1.0 - lam_init)


def _diff_attention_sample(q, k_new, v_new, cache_k, cache_v, page_table, rel_bias, lam_vec, g, lam_init):
    db, n_q, width = q.shape
    n_head = width // (2 * D_HA)
    n_page = page_table.shape[1]
    past = n_page * PAGE
    pad_new = lambda a: jnp.pad(a, ((0, 0), (0, PAGE - n_q), (0, 0)))
    qpos = past + jnp.arange(n_q, dtype=jnp.int32)
    kpos = jnp.arange(past + PAGE, dtype=jnp.int32)
    bias = rel_bias[_t5_bucket(qpos[:, None] - kpos[None, :])]
    bias = jnp.where((kpos[None, :] <= qpos[:, None])[..., None], bias, NEG_BIG)
    bias = jnp.transpose(bias, (0, 2, 1)).reshape(n_q * n_head, n_page + 1, PAGE)
    bias = jnp.transpose(bias, (1, 0, 2))
    seg = _segment_matrix(2 * n_head, D_HA, [(i % 2) * n_head + i // 2 for i in range(2 * n_head)])
    rows = n_q * n_head
    page = lambda b, p, pt: (pt[b, jnp.minimum(p, n_page - 1)], 0, 0)
    grid_spec = pltpu.PrefetchScalarGridSpec(
        num_scalar_prefetch=1,
        grid=(db, n_page + 1),
        in_specs=[pl.BlockSpec((None, n_q, width), lambda b, p, pt: (b, 0, 0)),
                  pl.BlockSpec((None, PAGE, width), page),
                  pl.BlockSpec((None, PAGE, width), page),
                  pl.BlockSpec((None, PAGE, width), lambda b, p, pt: (b, 0, 0)),
                  pl.BlockSpec((None, PAGE, width), lambda b, p, pt: (b, 0, 0)),
                  pl.BlockSpec((None, rows, PAGE), lambda b, p, pt: (p, 0, 0)),
                  pl.BlockSpec((2 * n_head, width), lambda b, p, pt: (0, 0)),
                  pl.BlockSpec((4, D_HA), lambda b, p, pt: (0, 0)),
                  pl.BlockSpec((1, 2 * D_HA), lambda b, p, pt: (0, 0))],
        out_specs=pl.BlockSpec((None, n_q, width), lambda b, p, pt: (b, 0, 0)),
        scratch_shapes=[pltpu.VMEM((2, rows, 1), F32), pltpu.VMEM((2, rows, 1), F32),
                        pltpu.VMEM((2, rows, width), F32)])
    return pl.pallas_call(
        functools.partial(_diff_sample_kernel, n_q, n_page, lam_init),
        grid_spec=grid_spec,
        out_shape=jax.ShapeDtypeStruct((db, n_q, width), F32),
        compiler_params=_cparams("parallel", "arbitrary"),
        name="diff_attn_sample",
    )(page_table, q, cache_k, cache_v, pad_new(k_new), pad_new(v_new), bias, seg, lam_vec, g.reshape(1, -1))


def _sb_sample_kernel(n_q, scale, pt_ref, q_ref, kc_ref, vc_ref, kn_ref, vn_ref, mask_ref, seg_ref, o_ref,
                      run_s, a_s):
    p = pl.program_id(1)
    n_head = seg_ref.shape[0]
    dh = vc_ref.shape[-1] // n_head
    keys = kc_ref.shape[0]
    suffix = _suffix_matrix(keys)

    def block(k_ref, v_ref, mask):
        k = k_ref[...]
        z = jnp.concatenate([_segment_scores(seg_ref[...], k * q_ref[i:i + 1, :]) for i in range(n_q)], 0) * scale
        ls, lk = _log_sigmoid_pair(z)
        if mask is not None:
            lk = lk * mask
        run = run_s[...]
        a = jnp.exp(ls + _dot_exact_rhs(lk, suffix) + run)
        if mask is not None:
            a = a * mask
        run_s[...] = run + jnp.sum(lk, -1, keepdims=True)
        a_s[...] += _dot(a.astype(BF16), v_ref[...].astype(BF16))

    @pl.when(p == 0)
    def _():
        run_s[...] = jnp.zeros(run_s.shape, F32)
        a_s[...] = jnp.zeros(a_s.shape, F32)
        block(kn_ref, vn_ref, mask_ref[...])

    @pl.when(p > 0)
    def _():
        block(kc_ref, vc_ref, None)

    @pl.when(p == pl.num_programs(1) - 1)
    def _():
        o_ref[...] = _head_block_sum(a_s[...], n_head, dh)


def _sb_attention_sample(q, k_new, v_new, cache_k, cache_v, page_table, dh):
    db, n_q, width = q.shape
    n_head = width // dh
    n_page = page_table.shape[1]
    pad_new = lambda a: jnp.pad(a, ((0, 0), (0, PAGE - n_q), (0, 0)))
    qi = jnp.repeat(jnp.arange(n_q, dtype=jnp.int32), n_head)[:, None]
    mask = (jnp.arange(PAGE, dtype=jnp.int32)[None, :] < qi).astype(F32)
    seg = _segment_matrix(n_head, dh, list(range(n_head)))
    rows = n_q * n_head
    page = lambda b, p, pt: (pt[b, n_page - jnp.maximum(p, 1)], 0, 0)
    grid_spec = pltpu.PrefetchScalarGridSpec(
        num_scalar_prefetch=1,
        grid=(db, n_page + 1),
        in_specs=[pl.BlockSpec((None, n_q, width), lambda b, p, pt: (b, 0, 0)),
                  pl.BlockSpec((None, PAGE, width), page),
                  pl.BlockSpec((None, PAGE, width), page),
                  pl.BlockSpec((None, PAGE, width), lambda b, p, pt: (b, 0, 0)),
                  pl.BlockSpec((None, PAGE, width), lambda b, p, pt: (b, 0, 0)),
                  pl.BlockSpec((rows, PAGE), lambda b, p, pt: (0, 0)),
                  pl.BlockSpec((n_head, width), lambda b, p, pt: (0, 0))],
        out_specs=pl.BlockSpec((None, n_q, width), lambda b, p, pt: (b, 0, 0)),
        scratch_shapes=[pltpu.VMEM((rows, 1), F32), pltpu.VMEM((rows, width), F32)])
    return pl.pallas_call(
        functools.partial(_sb_sample_kernel, n_q, dh ** -0.5),
        grid_spec=grid_spec,
        out_shape=jax.ShapeDtypeStruct((db, n_q, width), F32),
        compiler_params=_cparams("parallel", "arbitrary"),
        name="sb_attn_sample",
    )(page_table, q, cache_k, cache_v, pad_new(k_new), pad_new(v_new), mask, seg)


def _top_k_rows(s, k):
    n, lanes = s.shape
    row = lax.broadcasted_iota(jnp.int32, (n, lanes), 0).astype(F32)
    slot = lax.broadcasted_iota(jnp.int32, (k, lanes), 0)

    def body(i, carry):
        s, vals, ids = carry
        m = jnp.max(s, 0, keepdims=True)
        first = jnp.min(jnp.where(s == m, row, float(n)), 0, keepdims=True)
        vals = jnp.where(slot == i, m, vals)
        ids = jnp.where(slot == i, first, ids)
        return jnp.where(row == first, -jnp.inf, s), vals, ids

    _, vals, ids = lax.fori_loop(0, k, body, (s, jnp.zeros((k, lanes), F32), jnp.zeros((k, lanes), F32)))
    return vals, ids


def _outer_rows(a, b, fn):
    return jnp.concatenate([fn(a[i:i + 1, :], b) for i in range(a.shape[0])], 0)


def _peer_route_kernel(wqt_ref, keys_ref, x_ref, idx_ref, g_ref):
    nk = keys_ref.shape[2]
    qt = _dot_nt(wqt_ref[...], x_ref[...].astype(BF16)).astype(BF16)
    half = keys_ref.shape[3]
    for h in range(keys_ref.shape[0]):
        top = []
        for p in range(2):
            r0 = (h * 2 + p) * half
            s = _dot(keys_ref[h, p], qt[r0:r0 + half, :])
            top.append(_top_k_rows(s, PEER_TOPK))
        (s1, i1), (s2, i2) = top
        cand = _outer_rows(s1, s2, lambda a, b: a + b)
        expert = _outer_rows(i1, i2, lambda a, b: a * nk + b)
        sc, ci = _top_k_rows(cand, PEER_TOPK)
        crow = lax.broadcasted_iota(jnp.int32, cand.shape, 0).astype(F32)
        picked = [jnp.sum(jnp.where(crow == ci[j:j + 1, :], expert, 0.0), 0, keepdims=True) for j in range(PEER_TOPK)]
        e = jnp.exp(sc - jnp.max(sc, 0, keepdims=True))
        idx_ref[h] = jnp.concatenate(picked, 0).astype(jnp.int32)
        g_ref[h] = e / jnp.sum(e, 0, keepdims=True)


def _peer_route(x, wqt_bf16, subkeys_bf16):
    t, d = x.shape
    heads = subkeys_bf16.shape[0]
    tt = 128
    return pl.pallas_call(
        _peer_route_kernel,
        grid=(t // tt,),
        in_specs=[pl.BlockSpec(wqt_bf16.shape, lambda i: (0, 0)),
                  pl.BlockSpec(subkeys_bf16.shape, lambda i: (0, 0, 0, 0)),
                  pl.BlockSpec((tt, d), lambda i: (i, 0))],
        out_specs=[pl.BlockSpec((heads, PEER_TOPK, tt), lambda i: (0, 0, i)),
                   pl.BlockSpec((heads, PEER_TOPK, tt), lambda i: (0, 0, i))],
        out_shape=[jax.ShapeDtypeStruct((heads, PEER_TOPK, t), jnp.int32),
                   jax.ShapeDtypeStruct((heads, PEER_TOPK, t), F32)],
        compiler_params=_cparams("parallel"),
        name="peer_route",
    )(wqt_bf16, subkeys_bf16, x)


PEER_TOKEN_CHUNK = 1024
PEER_EXPERT_TILE = 256
PEER_PAIR_BLOCK = 1024
ROW_SUBLANES = 8


def _gelu(x):
    return 0.5 * x * (1.0 + lax.erf(x * (2.0 ** -0.5)))


def _peer_expert_kernel(meta_ref, ent_ref, gate_ref, x_ref, u_ref, v_ref, o_ref, stage, act):
    i = pl.program_id(0)
    first, count = meta_ref[2, i], meta_ref[3, i]

    @pl.when(first == 1)
    def _():
        o_ref[...] = jnp.zeros(o_ref.shape, F32)

    @pl.when(i == 0)
    def _():
        stage[...] = jnp.zeros(stage.shape, F32)

    @pl.when(count > 0)
    def _():
        _peer_block(ent_ref, gate_ref, x_ref, u_ref, v_ref, o_ref, stage, act, count)


def _peer_block(ent_ref, gate_ref, x_ref, u_ref, v_ref, o_ref, stage, act, count):
    n_pair = ent_ref.shape[-1]
    shift = PEER_EXPERT_TILE.bit_length() - 1
    lanes = stage.shape[-1]

    def unpack(j):
        ent = ent_ref[0, j]
        return lax.shift_right_logical(ent, shift), lax.bitwise_and(ent, PEER_EXPERT_TILE - 1)

    steps = (count + 7) // 8

    def dots(s, carry):
        for r in range(8):
            j = s * 8 + r
            tok, e = unpack(j)
            pr = x_ref[tok] * u_ref[e]
            acc = pr[:, 0:lanes]
            for c in range(1, pr.shape[-1] // lanes):
                acc = acc + pr[:, c * lanes:(c + 1) * lanes]
            stage[pl.ds(pl.multiple_of(j * 8, 8), 8), :] = acc
        return carry

    lax.fori_loop(0, steps, dots, 0)
    part = stage[pl.ds(0, n_pair, stride=8), :]
    for s in range(1, 8):
        part = part + stage[pl.ds(s, n_pair, stride=8), :]
    a = _gelu(jnp.sum(part, -1, keepdims=True))
    act[...] = jnp.broadcast_to(a, act.shape)

    def mix(s, carry):
        for r in range(8):
            j = s * 8 + r
            tok, e = unpack(j)
            w = act[pl.ds(j, 1), :] * gate_ref[0, j]
            o_ref[tok] = o_ref[tok] + v_ref[e] * w
        return carry

    lax.fori_loop(0, steps, mix, 0)


def _peer_plan(idx, t_pad):
    heads, topk, t = idx.shape
    n_tile = (PEER_NK * PEER_NK) // PEER_EXPERT_TILE
    n_chunk = t_pad // PEER_TOKEN_CHUNK
    n_cell = n_chunk * n_tile
    n_pairs = heads * topk * t
    n_blk = n_pairs // PEER_PAIR_BLOCK + n_cell
    e = idx.reshape(-1)
    tok = jnp.broadcast_to(jnp.arange(t, dtype=jnp.int32), (heads * topk, t)).reshape(-1)
    cell = (tok // PEER_TOKEN_CHUNK) * n_tile + e // PEER_EXPERT_TILE
    order = jnp.argsort(cell, stable=True).astype(jnp.int32)
    bounds = jnp.searchsorted(cell[order], jnp.arange(n_cell + 1, dtype=jnp.int32), side='left').astype(jnp.int32)
    cell_off, cnt = bounds[:-1], bounds[1:] - bounds[:-1]
    cell_blk = (cnt + PEER_PAIR_BLOCK - 1) // PEER_PAIR_BLOCK
    blk_end = jnp.cumsum(cell_blk)
    blk_id = jnp.arange(n_blk, dtype=jnp.int32)
    used = blk_id < blk_end[-1]
    blk_cell = jnp.minimum(jnp.searchsorted(blk_end, blk_id, side='right'), n_cell - 1).astype(jnp.int32)
    blk_cell = jnp.where(used, blk_cell, blk_cell[jnp.maximum(blk_end[-1] - 1, 0)])
    in_cell = (blk_id - (blk_end - cell_blk)[blk_cell]) * PEER_PAIR_BLOCK
    blk_cnt = jnp.where(used, jnp.clip(cnt[blk_cell] - in_cell, 0, PEER_PAIR_BLOCK), 0)
    chunk = blk_cell // n_tile
    first = jnp.concatenate([jnp.ones((1,), jnp.int32), (chunk[1:] != chunk[:-1]).astype(jnp.int32)])
    meta = jnp.stack([chunk, blk_cell % n_tile, first, blk_cnt]).astype(jnp.int32)
    lane = jnp.arange(PEER_PAIR_BLOCK, dtype=jnp.int32)[None, :]
    src = (cell_off[blk_cell] + in_cell)[:, None] + lane
    valid = lane < blk_cnt[:, None]
    src = order[jnp.where(valid, src, 0)]
    packed = (tok[src] % PEER_TOKEN_CHUNK) * PEER_EXPERT_TILE + e[src] % PEER_EXPERT_TILE
    return meta, jnp.where(valid, packed, 0)[:, None, :], src, valid


def _peer_experts(x, idx, gates, u, v):
    t, d = x.shape
    t_pad = -(-t // PEER_TOKEN_CHUNK) * PEER_TOKEN_CHUNK
    w = d // ROW_SUBLANES
    meta, ent, src, valid = _peer_plan(idx, t_pad)
    gate = jnp.where(valid, gates.reshape(-1)[src], 0.0)[:, None, :]
    n_blk = ent.shape[0]
    rows = lambda a: a.reshape(a.shape[0], ROW_SUBLANES, w)
    grid_spec = pltpu.PrefetchScalarGridSpec(
        num_scalar_prefetch=1,
        grid=(n_blk,),
        in_specs=[pl.BlockSpec((None, 1, PEER_PAIR_BLOCK), lambda i, m: (i, 0, 0), memory_space=pltpu.SMEM),
                  pl.BlockSpec((None, 1, PEER_PAIR_BLOCK), lambda i, m: (i, 0, 0), memory_space=pltpu.SMEM),
                  pl.BlockSpec((PEER_TOKEN_CHUNK, ROW_SUBLANES, w), lambda i, m: (m[0, i], 0, 0)),
                  pl.BlockSpec((PEER_EXPERT_TILE, ROW_SUBLANES, w), lambda i, m: (m[1, i], 0, 0)),
                  pl.BlockSpec((PEER_EXPERT_TILE, ROW_SUBLANES, w), lambda i, m: (m[1, i], 0, 0))],
        out_specs=pl.BlockSpec((PEER_TOKEN_CHUNK, ROW_SUBLANES, w), lambda i, m: (m[0, i], 0, 0)),
        scratch_shapes=[pltpu.VMEM((PEER_PAIR_BLOCK * 8, 128), F32), pltpu.VMEM((PEER_PAIR_BLOCK, w), F32)])
    out = pl.pallas_call(
        _peer_expert_kernel,
        grid_spec=grid_spec,
        out_shape=jax.ShapeDtypeStruct((t_pad, ROW_SUBLANES, w), F32),
        compiler_params=pltpu.CompilerParams(dimension_semantics=("arbitrary",),
                                             vmem_limit_bytes=56 * 1024 * 1024),
        name="peer_experts",
    )(meta, ent, gate, rows(jnp.pad(x, ((0, t_pad - t), (0, 0)))), rows(u), rows(v))
    return out.reshape(t_pad, d)[:t]


def _peer_layer(x, wq, subkeys, u, v, g, b, alpha):
    idx, gates = _peer_route(x, wq.T.astype(BF16), subkeys.astype(BF16))
    return _res_ln(x, _peer_experts(x, idx, gates, u, v), g, b, alpha)


def _pad_cols(w, mult):
    return jnp.pad(w, ((0, 0), (0, -w.shape[1] % mult)))


def kernel(x_prompt, x_sample, cache_diff_k, cache_diff_v, state_mlstm_C, state_mlstm_n, state_mlstm_m, state_mlstm_conv, cache_sb_k, cache_sb_v, page_table, rel_bias, w_in_even, b_ig, b_fg, lam_q1, lam_k1, lam_q2, lam_k2, diff_norm_g, conv_w, conv_b, w_q_mlstm, w_k_mlstm, mlstm_norm_g, mlstm_skip, w_out_even, w_qkv_odd, w_out_odd, ln_g, ln_b, peer_wq, peer_subkeys, peer_u, peer_v):
    nb, seq, d = x_prompt.shape
    db, n_q, _ = x_sample.shape
    tp, ts = nb * seq, db * n_q
    n_pool = cache_diff_k.shape[1]
    depth = ln_g.shape[0]
    alpha = (2.0 * depth) ** 0.25
    h_a, h_b, h_c = d // 256, w_q_mlstm.shape[1], d // 128
    w_a, w_b = h_a * 2 * D_HA, d // 2
    assert depth == 2 and w_in_even.shape[0] == 1 and w_qkv_odd.shape[0] == 1 and w_a == w_b
    lam_init = 0.8 - 0.6 * math.exp(-0.3 * 0)
    x = jnp.concatenate([x_prompt.reshape(tp, d), x_sample.reshape(ts, d)])

    n_main = 3 * w_a + 3 * w_b
    proj = _matmul(x, _pad_cols(w_in_even[0], 896).astype(BF16), 896)
    qa, ka, va, u_b = (proj[:, i * w_a:(i + 1) * w_a] for i in range(4))
    gates = proj[:, n_main:n_main + 2 * h_b]
    lam_vec = jnp.stack([lam_q1[0], lam_k1[0], lam_q2[0], lam_k2[0]])
    ya_p = _diff_attention_prompt(proj, nb, seq, h_a, rel_bias, lam_vec, diff_norm_g[0], lam_init, 256)
    smp = lambda a: a[tp:].reshape(db, n_q, -1)
    ya_s = _diff_attention_sample(smp(qa), smp(ka), smp(va), cache_diff_k[0].reshape(n_pool, PAGE, w_a),
                                  cache_diff_v[0].reshape(n_pool, PAGE, w_a), page_table, rel_bias, lam_vec,
                                  diff_norm_g[0], lam_init)
    mp = {'b_ig': b_ig[0], 'b_fg': b_fg[0], 'conv_w': conv_w[0], 'conv_b': conv_b[0], 'w_q_mlstm': w_q_mlstm[0],
          'w_k_mlstm': w_k_mlstm[0], 'mlstm_norm_g': mlstm_norm_g[0], 'mlstm_skip': mlstm_skip[0]}
    lane0 = 3 * w_a // 256
    yb_p, c_p, n_p, m_p = _mlstm(proj, gates[:tp], nb, seq, MLSTM_CHUNK, h_b, lane0,
                                 jnp.zeros((nb, 8, w_b), F32), jnp.zeros((nb, h_b, 256, 256), F32),
                                 jnp.zeros((nb, h_b, 256), F32), jnp.zeros((nb, h_b), F32), mp)
    s_pad = 8
    pad_t = lambda a, val=0.0: jnp.pad(a, ((0, 0), (0, s_pad - n_q), (0, 0)), constant_values=val)
    gate_s = smp(gates)
    gate_s = jnp.concatenate([pad_t(gate_s[..., :h_b], NEG_BIG), pad_t(gate_s[..., h_b:], -NEG_BIG)], -1)
    conv0 = jnp.pad(state_mlstm_conv[0], ((0, 0), (8 - (CONV_W - 1), 0), (0, 0)))
    yb_s, c_s, n_s, m_s = _mlstm(pad_t(smp(proj)).reshape(db * s_pad, -1), gate_s.reshape(db * s_pad, -1), db, s_pad,
                                 s_pad, h_b, lane0, conv0, state_mlstm_C[0], state_mlstm_n[0], state_mlstm_m[0], mp)
    yb_s = yb_s.reshape(db, s_pad, w_b)[:, :n_q].reshape(ts, w_b)
    y = jnp.concatenate([jnp.concatenate([ya_p, yb_p], -1), jnp.concatenate([ya_s.reshape(ts, w_a), yb_s], -1)])
    x = _matmul_res_ln(y, w_out_even[0].astype(BF16), x, ln_g[0, 0], ln_b[0, 0], alpha)
    x = _peer_layer(x, peer_wq[0], peer_subkeys[0], peer_u[0], peer_v[0], ln_g[0, 1], ln_b[0, 1], alpha)

    qkv = _matmul(x, w_qkv_odd[0].astype(BF16), 1024)
    y_p = _sb_attention_prompt(qkv, nb, seq, h_c, 128, 256)
    q_s, k_s, v_s = (smp(qkv[:, i * d:(i + 1) * d]) for i in range(3))
    y_s = _sb_attention_sample(q_s, k_s, v_s, cache_sb_k[0].reshape(n_pool, PAGE, d),
                               cache_sb_v[0].reshape(n_pool, PAGE, d), page_table, 128)
    y = jnp.concatenate([y_p, y_s.reshape(ts, d)])
    x = _matmul_res_ln(y, w_out_odd[0].astype(BF16), x, ln_g[1, 0], ln_b[1, 0], alpha)
    x = _peer_layer(x, peer_wq[1], peer_subkeys[1], peer_u[1], peer_v[1], ln_g[1, 1], ln_b[1, 1], alpha)

    prm = lambda a, *s: a[:tp].reshape((1, nb, seq) + s)
    u_p = proj[:tp, 3 * w_a:3 * w_a + w_b].reshape(nb, seq, w_b)
    u_s = smp(u_b)
    sbk, sbv = qkv[:, d:2 * d], qkv[:, 2 * d:]
    return (x[:tp].reshape(nb, seq, d), x[tp:].reshape(db, n_q, d),
            prm(ka, h_a, 2, D_HA), prm(va, h_a, 2 * D_HA),
            smp(ka).reshape(1, db, n_q, h_a, 2, D_HA), smp(va).reshape(1, db, n_q, h_a, 2 * D_HA),
            c_p[None], n_p[None], m_p[None], u_p[None, :, seq - (CONV_W - 1):],
            c_s[None], n_s[None], m_s[None], u_s[None, :, n_q - (CONV_W - 1):],
            prm(sbk, h_c, 128), prm(sbv, h_c, 128),
            smp(sbk).reshape(1, db, n_q, h_c, 128), smp(sbv).reshape(1, db, n_q, h_c, 128))
```

```python
import functools
import math

import jax
import jax.numpy as jnp
import numpy as np
from jax import lax
from jax.experimental import pallas as pl
from jax.experimental.pallas import tpu as pltpu

F32 = jnp.float32
BF16 = jnp.bfloat16

LN_EPS = 1e-5
NEG_BIG = -1e30
VMEM_LIMIT = 48 * 1024 * 1024

D_HA = 64
N_BUCKETS = 32
MAX_DIST = 128
CONV_W = 4
MLSTM_CHUNK = 128
PAGE = 128
PEER_HEADS = 8
PEER_NK = 128
PEER_TOPK = 16


def _cparams(*sem):
    return pltpu.CompilerParams(dimension_semantics=sem, vmem_limit_bytes=VMEM_LIMIT)


def _dot(a, b):
    return jnp.dot(a, b, preferred_element_type=F32)


def _dot_nt(a, b):
    return lax.dot_general(a, b, (((1,), (1,)), ((), ())), preferred_element_type=F32)


def _dot_tn(a, b):
    return lax.dot_general(a, b, (((0,), (0,)), ((), ())), preferred_element_type=F32)


def _split3(x):
    h1 = x.astype(BF16)
    r1 = x - h1.astype(F32)
    h2 = r1.astype(BF16)
    h3 = (r1 - h2.astype(F32)).astype(BF16)
    return h1, h2, h3


def _dot_exact_rhs(x, m_bf16):
    h1, h2, h3 = _split3(x)
    return _dot(h1, m_bf16) + _dot(h2, m_bf16) + _dot(h3, m_bf16)


def _row_tile(m):
    for t in (384, 256, 128, 64, 32, 16, 8):
        if m % t == 0:
            return t
    raise ValueError(f"row count {m} is not a multiple of 8")


def _mm_kernel(x_ref, w_ref, o_ref):
    o_ref[...] = _dot(x_ref[...].astype(BF16), w_ref[...])


def _matmul(x, w_bf16, tn):
    m, k = x.shape
    n = w_bf16.shape[1]
    tm = _row_tile(m)
    return pl.pallas_call(
        _mm_kernel,
        grid=(m // tm, n // tn),
        in_specs=[pl.BlockSpec((tm, k), lambda i, j: (i, 0)),
                  pl.BlockSpec((k, tn), lambda i, j: (0, j))],
        out_specs=pl.BlockSpec((tm, tn), lambda i, j: (i, j)),
        out_shape=jax.ShapeDtypeStruct((m, n), F32),
        compiler_params=_cparams("parallel", "arbitrary"),
        name="matmul",
    )(x, w_bf16)


def _mm_nt_kernel(w_ref, x_ref, o_ref):
    o_ref[...] = _dot_nt(w_ref[...], x_ref[...].astype(BF16))


def _matmul_t(wt_bf16, x):
    n, k = wt_bf16.shape
    m = x.shape[0]
    tm = _row_tile(m)
    return pl.pallas_call(
        _mm_nt_kernel,
        grid=(m // tm,),
        in_specs=[pl.BlockSpec((n, k), lambda i: (0, 0)),
                  pl.BlockSpec((tm, k), lambda i: (i, 0))],
        out_specs=pl.BlockSpec((n, tm), lambda i: (0, i)),
        out_shape=jax.ShapeDtypeStruct((n, m), F32),
        compiler_params=_cparams("parallel"),
        name="matmul_t",
    )(wt_bf16, x)


def _layer_norm(xf, g, b):
    mu = jnp.mean(xf, -1, keepdims=True)
    xc = xf - mu
    var = jnp.mean(xc * xc, -1, keepdims=True)
    return xc * lax.rsqrt(var + LN_EPS) * g + b


def _mm_res_ln_kernel(alpha, a_ref, w_ref, x_ref, g_ref, b_ref, o_ref):
    y = _dot(a_ref[...].astype(BF16), w_ref[...])
    o_ref[...] = _layer_norm(alpha * x_ref[...] + y, g_ref[...], b_ref[...])


def _matmul_res_ln(a, w_bf16, x, g, b, alpha):
    m, k = a.shape
    n = w_bf16.shape[1]
    tm = _row_tile(m)
    return pl.pallas_call(
        functools.partial(_mm_res_ln_kernel, alpha),
        grid=(m // tm,),
        in_specs=[pl.BlockSpec((tm, k), lambda i: (i, 0)),
                  pl.BlockSpec((k, n), lambda i: (0, 0)),
                  pl.BlockSpec((tm, n), lambda i: (i, 0)),
                  pl.BlockSpec((1, n), lambda i: (0, 0)),
                  pl.BlockSpec((1, n), lambda i: (0, 0))],
        out_specs=pl.BlockSpec((tm, n), lambda i: (i, 0)),
        out_shape=jax.ShapeDtypeStruct((m, n), F32),
        compiler_params=_cparams("parallel"),
        name="matmul_res_ln",
    )(a, w_bf16, x, g.reshape(1, n), b.reshape(1, n))


def _res_ln_kernel(alpha, x_ref, f_ref, g_ref, b_ref, o_ref):
    o_ref[...] = _layer_norm(alpha * x_ref[...] + f_ref[...], g_ref[...], b_ref[...])


def _res_ln(x, f, g, b, alpha):
    m, n = x.shape
    tm = _row_tile(m)
    return pl.pallas_call(
        functools.partial(_res_ln_kernel, alpha),
        grid=(m // tm,),
        in_specs=[pl.BlockSpec((tm, n), lambda i: (i, 0)),
                  pl.BlockSpec((tm, n), lambda i: (i, 0)),
                  pl.BlockSpec((1, n), lambda i: (0, 0)),
                  pl.BlockSpec((1, n), lambda i: (0, 0))],
        out_specs=pl.BlockSpec((tm, n), lambda i: (i, 0)),
        out_shape=jax.ShapeDtypeStruct((m, n), F32),
        compiler_params=_cparams("parallel"),
        name="res_ln",
    )(x, f, g.reshape(1, n), b.reshape(1, n))


def _t5_bucket(dist):
    n = jnp.maximum(dist, 0)
    exact = N_BUCKETS // 2
    large = exact + (jnp.log(jnp.maximum(n, exact).astype(F32) / exact)
                     / math.log(MAX_DIST / exact) * (N_BUCKETS - exact)).astype(jnp.int32)
    return jnp.where(n < exact, n, jnp.minimum(large, N_BUCKETS - 1))


def _lambda(lam_ref, lam_init):
    lq1, lk1, lq2, lk2 = lam_ref[0:1, :], lam_ref[1:2, :], lam_ref[2:3, :], lam_ref[3:4, :]
    return (jnp.exp(jnp.sum(lq1 * lk1, keepdims=True)) - jnp.exp(jnp.sum(lq2 * lk2, keepdims=True))
            + lam_init)


def _diff_finish(acc1, l1, acc2, l2, lam, g, lam_init):
    o = acc1 / l1 - lam * (acc2 / l2)
    ms = jnp.mean(o * o, -1, keepdims=True)
    return o * lax.rsqrt(ms + LN_EPS) * g * (1.0 - lam_init)


def _diff_prompt_kernel(tq, lam_init, q_ref, k_ref, v_ref, bias_ref, far_ref, lam_ref, g_ref, o_ref):
    qi = pl.program_id(2)
    scale = D_HA ** -0.5
    q = q_ref[...]
    lane = lax.broadcasted_iota(jnp.int32, q.shape, 1)
    q1 = jnp.where(lane < D_HA, q, 0.0).astype(BF16)
    q2 = jnp.where(lane >= D_HA, q, 0.0).astype(BF16)
    dv = v_ref.shape[-1]

    def update(state, ki, bias, causal):
        m1, l1, a1, m2, l2, a2 = state
        start = pl.multiple_of(ki * tq, tq)
        kb = k_ref[pl.ds(start, tq), :].astype(BF16)
        vb = v_ref[pl.ds(start, tq), :].astype(BF16)
        out = []
        for qm, m, l, a in ((q1, m1, l1, a1), (q2, m2, l2, a2)):
            s = _dot_nt(qm, kb) * scale + bias
            if causal:
                r = lax.broadcasted_iota(jnp.int32, s.shape, 0)
                c = lax.broadcasted_iota(jnp.int32, s.shape, 1)
                s = jnp.where(c <= r, s, NEG_BIG)
            mn = jnp.maximum(m, jnp.max(s, -1, keepdims=True))
            p = jnp.exp(s - mn)
            alpha = jnp.exp(m - mn)
            out += [mn, alpha * l + jnp.sum(p, -1, keepdims=True), alpha * a + _dot(p.astype(BF16), vb)]
        return tuple(out)

    init = (jnp.full((tq, 1), NEG_BIG, F32), jnp.zeros((tq, 1), F32), jnp.zeros((tq, dv), F32)) * 2
    state = update(init, qi, bias_ref[0], True)
    sub = jnp.where(qi >= 1, bias_ref[1], NEG_BIG)
    state = update(state, jnp.maximum(qi - 1, 0), sub, False)
    far = far_ref[...]
    m1, l1, a1, m2, l2, a2 = lax.fori_loop(0, jnp.maximum(qi - 1, 0),
                                           lambda ki, st: update(st, ki, far, False), state)
    o_ref[...] = _diff_finish(a1, l1, a2, l2, _lambda(lam_ref, lam_init), g_ref[...], lam_init)


def _diff_bias_tables(rel_bias, tq):
    h = rel_bias.shape[1]
    r = jnp.arange(tq, dtype=jnp.int32)[:, None]
    c = jnp.arange(tq, dtype=jnp.int32)[None, :]
    tiles = jnp.stack([rel_bias[_t5_bucket(r - c)], rel_bias[_t5_bucket(r - c + tq)]])
    return jnp.transpose(tiles, (3, 0, 1, 2)), rel_bias[N_BUCKETS - 1].reshape(h, 1, 1)


def _diff_attention_prompt(proj, batch, seq, n_head, rel_bias, lam_vec, g, lam_init, tq):
    assert tq >= MAX_DIST and seq % tq == 0
    dh = 2 * D_HA
    nq = seq // tq
    bias_tiles, far = _diff_bias_tables(rel_bias, tq)
    return pl.pallas_call(
        functools.partial(_diff_prompt_kernel, tq, lam_init),
        grid=(batch, n_head, nq),
        in_specs=[pl.BlockSpec((tq, dh), lambda b, h, i: (b * nq + i, h)),
                  pl.BlockSpec((seq, dh), lambda b, h, i: (b, n_head + h)),
                  pl.BlockSpec((seq, dh), lambda b, h, i: (b, 2 * n_head + h)),
                  pl.BlockSpec((None, 2, tq, tq), lambda b, h, i: (h, 0, 0, 0)),
                  pl.BlockSpec((None, 1, 1), lambda b, h, i: (h, 0, 0)),
                  pl.BlockSpec((4, D_HA), lambda b, h, i: (0, 0)),
                  pl.BlockSpec((1, dh), lambda b, h, i: (0, 0))],
        out_specs=pl.BlockSpec((tq, dh), lambda b, h, i: (b * nq + i, h)),
        out_shape=jax.ShapeDtypeStruct((batch * seq, n_head * dh), F32),
        compiler_params=_cparams("parallel", "parallel", "arbitrary"),
        name="diff_attn_prompt",
    )(proj, proj, proj, bias_tiles, far, lam_vec, g.reshape(1, dh))


def _log_sigmoid_pair(z):
    t = jnp.log(1.0 + jnp.exp(-jnp.abs(z)))
    return jnp.minimum(z, 0.0) - t, jnp.minimum(-z, 0.0) - t


def _suffix_matrix(n):
    j = lax.broadcasted_iota(jnp.int32, (n, n), 0)
    s = lax.broadcasted_iota(jnp.int32, (n, n), 1)
    return jnp.where(j > s, 1.0, 0.0).astype(BF16)


def _sb_prompt_kernel(tq, scale, q_ref, k_ref, v_ref, o_ref):
    qi = pl.program_id(2)
    q = q_ref[...].astype(BF16)
    dv = v_ref.shape[-1]
    suffix = _suffix_matrix(tq)

    def update(state, ki, causal):
        run, acc = state
        start = pl.multiple_of(ki * tq, tq)
        kb = k_ref[pl.ds(start, tq), :].astype(BF16)
        vb = v_ref[pl.ds(start, tq), :].astype(BF16)
        z = _dot_nt(q, kb) * scale
        ls, lk = _log_sigmoid_pair(z)
        if causal:
            r = lax.broadcasted_iota(jnp.int32, z.shape, 0)
            c = lax.broadcasted_iota(jnp.int32, z.shape, 1)
            mask = c < r
            lk = jnp.where(mask, lk, 0.0)
        after = _dot_exact_rhs(lk, suffix) + run
        a = jnp.exp(ls + after)
        if causal:
            a = jnp.where(mask, a, 0.0)
        return run + jnp.sum(lk, -1, keepdims=True), acc + _dot(a.astype(BF16), vb)

    state = update((jnp.zeros((tq, 1), F32), jnp.zeros((tq, dv), F32)), qi, True)
    _, acc = lax.fori_loop(0, qi, lambda j, st: update(st, qi - 1 - j, False), state)
    o_ref[...] = acc


def _sb_attention_prompt(qkv, batch, seq, n_head, dh, tq):
    nq = seq // tq
    return pl.pallas_call(
        functools.partial(_sb_prompt_kernel, tq, dh ** -0.5),
        grid=(batch, n_head, nq),
        in_specs=[pl.BlockSpec((tq, dh), lambda b, h, i: (b * nq + i, h)),
                  pl.BlockSpec((seq, dh), lambda b, h, i: (b, n_head + h)),
                  pl.BlockSpec((seq, dh), lambda b, h, i: (b, 2 * n_head + h))],
        out_specs=pl.BlockSpec((tq, dh), lambda b, h, i: (b * nq + i, h)),
        out_shape=jax.ShapeDtypeStruct((batch * seq, n_head * dh), F32),
        compiler_params=_cparams("parallel", "parallel", "arbitrary"),
        name="sb_attn_prompt",
    )(qkv, qkv, qkv)


def _mlstm_kernel(L, u_ref, v_ref, ob_ref, grow_ref, bcol_ref, conv0_ref, cw_ref, cb_ref,
                  wq_ref, wk_ref, c0_ref, n0_ref, m0_ref, g_ref, skip_ref,
                  y_ref, c_out, n_out, m_out, ext, c_s, n_s, m_s):
    c = pl.program_id(2)
    dk = u_ref.shape[-1]

    @pl.when(c == 0)
    def _():
        ext[0:8, :] = conv0_ref[...]
        c_s[...] = c0_ref[...]
        n_s[...] = n0_ref[...]
        m_s[...] = m0_ref[...]

    ext[8:8 + L, :] = u_ref[...]
    uc = cb_ref[...]
    for j in range(CONV_W):
        uc = uc + ext[pl.ds(8 - (CONV_W - 1) + j, L), :] * cw_ref[j:j + 1, :]
    ext[0:8, :] = ext[L:L + 8, :]
    ua = uc * jax.nn.sigmoid(uc)
    uab = ua.astype(BF16)
    qf = _dot(uab, wq_ref[...])
    q = qf.astype(BF16)
    kf = _dot(uab, wk_ref[...]) * dk ** -0.5
    k = kf.astype(BF16)
    vb = v_ref[...].astype(BF16)

    gr = grow_ref[...] + bcol_ref[...]
    li_r, lf_r = gr[0:1, :], _log_sigmoid_pair(gr[1:2, :])[0]
    t = lax.broadcasted_iota(jnp.int32, (L, L), 0)
    s = lax.broadcasted_iota(jnp.int32, (L, L), 1)
    causal = s <= t
    li_c = jnp.sum(jnp.where(s == t, li_r, 0.0), -1, keepdims=True)
    b_c = _dot_exact_rhs(jnp.where(causal, lf_r, 0.0), jnp.ones((L, 8), BF16))[:, 0:1]
    b_r = _dot_exact_rhs(jnp.broadcast_to(lf_r, (8, L)), jnp.where(t <= s, 1.0, 0.0).astype(BF16))[0:1, :]

    m = m_s[:, 0:1]
    cmat = c_s[...]
    nrow = n_s[...]
    d = jnp.where(causal, b_c - b_r + li_r, NEG_BIG)
    inter = b_c + m
    mt = jnp.maximum(inter, jnp.max(d, -1, keepdims=True))
    w = jnp.exp(d - mt) * _dot_nt(q, k)
    sc = jnp.exp(inter - mt)
    num = _dot(w.astype(BF16), vb) + sc * _dot(q, cmat.astype(BF16))
    den = jnp.sum(w, -1, keepdims=True) + sc * jnp.sum(qf * nrow, -1, keepdims=True)
    h = num / jnp.maximum(jnp.abs(den), jnp.exp(-mt))

    m_new = mt[L - 1:L, :]
    b_last = b_c[L - 1:L, :]
    wl = jnp.exp(b_last - b_c + li_c - m_new)
    dec = jnp.exp(b_last + m - m_new)
    kw = kf * wl
    c_s[...] = dec * cmat + _dot_tn(kw.astype(BF16), vb)
    n_s[...] = dec * nrow + jnp.sum(kw, 0, keepdims=True)
    m_s[...] = jnp.broadcast_to(m_new, m_s.shape)

    mu = jnp.mean(h, -1, keepdims=True)
    hc = h - mu
    var = jnp.mean(hc * hc, -1, keepdims=True)
    hn = hc * lax.rsqrt(var + LN_EPS) * g_ref[...]
    y_ref[...] = jax.nn.sigmoid(ob_ref[...]) * (hn + skip_ref[...] * ua)

    @pl.when(c == pl.num_programs(2) - 1)
    def _():
        c_out[...] = c_s[...]
        n_out[...] = n_s[...]
        m_out[...] = m_s[...]


def _mlstm(rows, gates, batch, seq, L, n_head, lane0, conv0, c0, n0, m0, p):
    dk = 256
    nc = seq // L
    t_rows = batch * seq
    grow = jnp.transpose(gates.reshape(batch * nc, L, 2, n_head), (3, 0, 2, 1))
    bias = jnp.stack([p['b_ig'], p['b_fg']], -1)
    outs = pl.pallas_call(
        functools.partial(_mlstm_kernel, L),
        grid=(batch, n_head, nc),
        in_specs=[pl.BlockSpec((L, dk), lambda b, h, c: (b * nc + c, lane0 + h)),
                  pl.BlockSpec((L, dk), lambda b, h, c: (b * nc + c, lane0 + n_head + h)),
                  pl.BlockSpec((L, dk), lambda b, h, c: (b * nc + c, lane0 + 2 * n_head + h)),
                  pl.BlockSpec((None, None, 2, L), lambda b, h, c: (h, b * nc + c, 0, 0)),
                  pl.BlockSpec((None, 2, 1), lambda b, h, c: (h, 0, 0)),
                  pl.BlockSpec((None, 8, dk), lambda b, h, c: (b, 0, h)),
                  pl.BlockSpec((CONV_W, dk), lambda b, h, c: (0, h)),
                  pl.BlockSpec((1, dk), lambda b, h, c: (0, h)),
                  pl.BlockSpec((None, dk, dk), lambda b, h, c: (h, 0, 0)),
                  pl.BlockSpec((None, dk, dk), lambda b, h, c: (h, 0, 0)),
                  pl.BlockSpec((None, None, dk, dk), lambda b, h, c: (b, h, 0, 0)),
                  pl.BlockSpec((None, None, 1, dk), lambda b, h, c: (b, h, 0, 0)),
                  pl.BlockSpec((None, None, 1, 128), lambda b, h, c: (b, h, 0, 0)),
                  pl.BlockSpec((1, dk), lambda b, h, c: (0, h)),
                  pl.BlockSpec((1, dk), lambda b, h, c: (0, h))],
        out_specs=[pl.BlockSpec((L, dk), lambda b, h, c: (b * nc + c, h)),
                   pl.BlockSpec((None, None, dk, dk), lambda b, h, c: (b, h, 0, 0)),
                   pl.BlockSpec((None, None, 1, dk), lambda b, h, c: (b, h, 0, 0)),
                   pl.BlockSpec((None, None, 1, 128), lambda b, h, c: (b, h, 0, 0))],
        out_shape=[jax.ShapeDtypeStruct((t_rows, n_head * dk), F32),
                   jax.ShapeDtypeStruct((batch, n_head, dk, dk), F32),
                   jax.ShapeDtypeStruct((batch, n_head, 1, dk), F32),
                   jax.ShapeDtypeStruct((batch, n_head, 1, 128), F32)],
        scratch_shapes=[pltpu.VMEM((L + 8, dk), F32), pltpu.VMEM((dk, dk), F32),
                        pltpu.VMEM((1, dk), F32), pltpu.VMEM((1, 128), F32)],
        compiler_params=_cparams("parallel", "parallel", "arbitrary"),
        name="mlstm",
    )(rows, rows, rows, grow, bias.reshape(n_head, 2, 1), conv0,
      p['conv_w'], p['conv_b'].reshape(1, -1), p['w_q_mlstm'].astype(BF16), p['w_k_mlstm'].astype(BF16),
      c0, n0.reshape(batch, n_head, 1, dk), jnp.broadcast_to(m0[:, :, None, None], (batch, n_head, 1, 128)),
      p['mlstm_norm_g'].reshape(1, -1), p['mlstm_skip'].reshape(1, -1))
    y, c_new, n_new, m_new = outs
    return y, c_new, n_new.reshape(batch, n_head, dk), m_new[:, :, 0, 0]


def _segment_matrix(n_seg, width, order):
    rows = np.zeros((n_seg, n_seg * width), np.float32)
    for i in range(n_seg):
        rows[order[i], i * width:(i + 1) * width] = 1.0
    return jnp.asarray(rows, BF16)


def _segment_scores(seg, kq):
    h1, h2, _ = _split3(kq)
    return _dot_nt(seg, h1) + _dot_nt(seg, h2)


def _key_head_matrices(n_key, n_head, rows):
    col = np.arange(n_key * n_head)
    spread = (col[None, :] // n_head == np.arange(n_key)[:, None]).astype(np.float32)
    own = (col[None, :] % n_head == (np.arange(rows) % n_head)[:, None]).astype(np.float32)
    return jnp.asarray(spread, BF16), jnp.asarray(spread.T, BF16), jnp.asarray(own)


def _spread_dot(a, own, spread, v2):
    return _dot((_dot(a.astype(BF16), spread) * own).astype(BF16), v2)


def _diff_sample_kernel(n_q, n_page, lam_init, pt_ref, q_ref, kc_ref, vc_ref, kn_ref, vn_ref, bias_ref, seg_ref,
                        own_ref, spread_ref, lam_ref, g_ref, o_ref, m_s, l_s, a_s):
    p = pl.program_id(1)
    n_head = seg_ref.shape[0] // 2
    scale = D_HA ** -0.5

    @pl.when(p == 0)
    def _():
        m_s[...] = jnp.full(m_s.shape, NEG_BIG, F32)
        l_s[...] = jnp.zeros(l_s.shape, F32)
        a_s[...] = jnp.zeros(a_s.shape, F32)

    def block(k_ref, v_ref):
        k = k_ref[...]
        sq = [_segment_scores(seg_ref[...], k * q_ref[i:i + 1, :]) for i in range(n_q)]
        v2 = v_ref[...].reshape(-1, v_ref.shape[-1]).astype(BF16)
        for mi in range(2):
            s = jnp.concatenate([x[mi * n_head:(mi + 1) * n_head] for x in sq], 0) * scale + bias_ref[...]
            m = m_s[mi]
            mn = jnp.maximum(m, jnp.max(s, -1, keepdims=True))
            pr = jnp.exp(s - mn)
            alpha = jnp.exp(m - mn)
            m_s[mi] = mn
            l_s[mi] = alpha * l_s[mi] + jnp.sum(pr, -1, keepdims=True)
            a_s[mi] = alpha * a_s[mi] + _spread_dot(pr, own_ref[...], spread_ref[...], v2)

    @pl.when(p < n_page)
    def _():
        block(kc_ref, vc_ref)

    @pl.when(p == n_page)
    def _():
        block(kn_ref, vn_ref)
        o_ref[...] = _diff_finish(a_s[0], l_s[0], a_s[1], l_s[1], _lambda(lam_ref, lam_init), g_ref[...], lam_init)


def _diff_attention_sample(q, k_new, v_new, cache_k, cache_v, page_table, rel_bias, lam_vec, g, lam_init):
    db, n_q, width = q.shape
    n_head = width // (2 * D_HA)
    dv = cache_v.shape[-1]
    n_page = page_table.shape[1]
    past = n_page * PAGE
    pad_new = lambda a: jnp.pad(a, ((0, 0), (0, PAGE - n_q), (0, 0)))
    qpos = past + jnp.arange(n_q, dtype=jnp.int32)
    kpos = jnp.arange(past + PAGE, dtype=jnp.int32)
    bias = rel_bias[_t5_bucket(qpos[:, None] - kpos[None, :])]
    bias = jnp.where((kpos[None, :] <= qpos[:, None])[..., None], bias, NEG_BIG)
    bias = jnp.transpose(bias, (0, 2, 1)).reshape(n_q * n_head, n_page + 1, PAGE)
    bias = jnp.transpose(bias, (1, 0, 2))
    seg = _segment_matrix(2 * n_head, D_HA, [(i % 2) * n_head + i // 2 for i in range(2 * n_head)])
    rows = n_q * n_head
    spread, _, own = _key_head_matrices(PAGE, n_head, rows)
    page = lambda b, p, pt: (pt[b, jnp.minimum(p, n_page - 1)], 0, 0)
    page4 = lambda b, p, pt: (pt[b, jnp.minimum(p, n_page - 1)], 0, 0, 0)
    const = lambda b, p, pt: (0, 0)
    grid_spec = pltpu.PrefetchScalarGridSpec(
        num_scalar_prefetch=1,
        grid=(db, n_page + 1),
        in_specs=[pl.BlockSpec((None, n_q, width), lambda b, p, pt: (b, 0, 0)),
                  pl.BlockSpec((None, PAGE, width), page),
                  pl.BlockSpec((None, PAGE, n_head, dv), page4),
                  pl.BlockSpec((None, PAGE, width), lambda b, p, pt: (b, 0, 0)),
                  pl.BlockSpec((None, PAGE, n_head, dv), lambda b, p, pt: (b, 0, 0, 0)),
                  pl.BlockSpec((None, rows, PAGE), lambda b, p, pt: (p, 0, 0)),
                  pl.BlockSpec(seg.shape, const),
                  pl.BlockSpec(own.shape, const),
                  pl.BlockSpec(spread.shape, const),
                  pl.BlockSpec((4, D_HA), const),
                  pl.BlockSpec((1, dv), const)],
        out_specs=pl.BlockSpec((None, rows, dv), lambda b, p, pt: (b, 0, 0)),
        scratch_shapes=[pltpu.VMEM((2, rows, 1), F32), pltpu.VMEM((2, rows, 1), F32),
                        pltpu.VMEM((2, rows, dv), F32)])
    out = pl.pallas_call(
        functools.partial(_diff_sample_kernel, n_q, n_page, lam_init),
        grid_spec=grid_spec,
        out_shape=jax.ShapeDtypeStruct((db, rows, dv), F32),
        compiler_params=_cparams("parallel", "arbitrary"),
        name="diff_attn_sample",
    )(page_table, q, cache_k, cache_v, pad_new(k_new), pad_new(v_new).reshape(db, PAGE, n_head, dv), bias, seg,
      own, spread, lam_vec, g.reshape(1, -1))
    return out.reshape(db, n_q, width)


def _sb_sample_kernel(scale, pt_ref, q_ref, kc_ref, vc_ref, kn_ref, vn_ref, mask_ref, own_ref, spread_ref,
                      collapse_ref, o_ref, run_s, a_s):
    p = pl.program_id(1)
    keys, _, dh = kc_ref.shape
    suffix = _suffix_matrix(keys)
    q = q_ref[...].astype(BF16)

    def block(k_ref, v_ref, mask):
        k2 = k_ref[...].reshape(-1, dh).astype(BF16)
        v2 = v_ref[...].reshape(-1, dh).astype(BF16)
        own = own_ref[...]
        z = _dot_exact_rhs(_dot_nt(q, k2) * own, collapse_ref[...]) * scale
        ls, lk = _log_sigmoid_pair(z)
        if mask is not None:
            lk = lk * mask
        run = run_s[...]
        a = jnp.exp(ls + _dot_exact_rhs(lk, suffix) + run)
        if mask is not None:
            a = a * mask
        run_s[...] = run + jnp.sum(lk, -1, keepdims=True)
        a_s[...] += _spread_dot(a, own, spread_ref[...], v2)

    @pl.when(p == 0)
    def _():
        run_s[...] = jnp.zeros(run_s.shape, F32)
        a_s[...] = jnp.zeros(a_s.shape, F32)
        block(kn_ref, vn_ref, mask_ref[...])

    @pl.when(p > 0)
    def _():
        block(kc_ref, vc_ref, None)

    @pl.when(p == pl.num_programs(1) - 1)
    def _():
        o_ref[...] = a_s[...]


def _sb_attention_sample(q, k_new, v_new, cache_k, cache_v, page_table):
    db, n_q, width = q.shape
    n_head, dh = cache_k.shape[2:]
    n_page = page_table.shape[1]
    pad_new = lambda a: jnp.pad(a, ((0, 0), (0, PAGE - n_q), (0, 0))).reshape(db, PAGE, n_head, dh)
    qi = jnp.repeat(jnp.arange(n_q, dtype=jnp.int32), n_head)[:, None]
    mask = (jnp.arange(PAGE, dtype=jnp.int32)[None, :] < qi).astype(F32)
    rows = n_q * n_head
    spread, collapse, own = _key_head_matrices(PAGE, n_head, rows)
    page = lambda b, p, pt: (pt[b, n_page - jnp.maximum(p, 1)], 0, 0, 0)
    const = lambda b, p, pt: (0, 0)
    grid_spec = pltpu.PrefetchScalarGridSpec(
        num_scalar_prefetch=1,
        grid=(db, n_page + 1),
        in_specs=[pl.BlockSpec((None, rows, dh), lambda b, p, pt: (b, 0, 0)),
                  pl.BlockSpec((None, PAGE, n_head, dh), page),
                  pl.BlockSpec((None, PAGE, n_head, dh), page),
                  pl.BlockSpec((None, PAGE, n_head, dh), lambda b, p, pt: (b, 0, 0, 0)),
                  pl.BlockSpec((None, PAGE, n_head, dh), lambda b, p, pt: (b, 0, 0, 0)),
                  pl.BlockSpec(mask.shape, const),
                  pl.BlockSpec(own.shape, const),
                  pl.BlockSpec(spread.shape, const),
                  pl.BlockSpec(collapse.shape, const)],
        out_specs=pl.BlockSpec((None, rows, dh), lambda b, p, pt: (b, 0, 0)),
        scratch_shapes=[pltpu.VMEM((rows, 1), F32), pltpu.VMEM((rows, dh), F32)])
    out = pl.pallas_call(
        functools.partial(_sb_sample_kernel, dh ** -0.5),
        grid_spec=grid_spec,
        out_shape=jax.ShapeDtypeStruct((db, rows, dh), F32),
        compiler_params=_cparams("parallel", "arbitrary"),
        name="sb_attn_sample",
    )(page_table, q.reshape(db, rows, dh), cache_k, cache_v, pad_new(k_new), pad_new(v_new), mask, own, spread,
      collapse)
    return out.reshape(db, n_q, width)


def _top_k_rows(s, k):
    n, lanes = s.shape
    row = lax.broadcasted_iota(jnp.int32, (n, lanes), 0).astype(F32)
    slot = lax.broadcasted_iota(jnp.int32, (k, lanes), 0)

    def body(i, carry):
        s, vals, ids = carry
        m = jnp.max(s, 0, keepdims=True)
        first = jnp.min(jnp.where(s == m, row, float(n)), 0, keepdims=True)
        vals = jnp.where(slot == i, m, vals)
        ids = jnp.where(slot == i, first, ids)
        return jnp.where(row == first, -jnp.inf, s), vals, ids

    _, vals, ids = lax.fori_loop(0, k, body, (s, jnp.zeros((k, lanes), F32), jnp.zeros((k, lanes), F32)))
    return vals, ids


def _outer_rows(a, b, fn):
    return jnp.concatenate([fn(a[i:i + 1, :], b) for i in range(a.shape[0])], 0)


def _peer_route_kernel(wqt_ref, keys_ref, x_ref, idx_ref, g_ref):
    nk = keys_ref.shape[2]
    qt = _dot_nt(wqt_ref[...], x_ref[...].astype(BF16)).astype(BF16)
    half = keys_ref.shape[3]
    for h in range(keys_ref.shape[0]):
        top = []
        for p in range(2):
            r0 = (h * 2 + p) * half
            s = _dot(keys_ref[h, p], qt[r0:r0 + half, :])
            top.append(_top_k_rows(s, PEER_TOPK))
        (s1, i1), (s2, i2) = top
        cand = _outer_rows(s1, s2, lambda a, b: a + b)
        expert = _outer_rows(i1, i2, lambda a, b: a * nk + b)
        sc, ci = _top_k_rows(cand, PEER_TOPK)
        crow = lax.broadcasted_iota(jnp.int32, cand.shape, 0).astype(F32)
        picked = [jnp.sum(jnp.where(crow == ci[j:j + 1, :], expert, 0.0), 0, keepdims=True) for j in range(PEER_TOPK)]
        e = jnp.exp(sc - jnp.max(sc, 0, keepdims=True))
        idx_ref[h] = jnp.concatenate(picked, 0).astype(jnp.int32)
        g_ref[h] = e / jnp.sum(e, 0, keepdims=True)


def _peer_route(x, wqt_bf16, subkeys_bf16):
    t, d = x.shape
    heads = subkeys_bf16.shape[0]
    tt = 128
    return pl.pallas_call(
        _peer_route_kernel,
        grid=(t // tt,),
        in_specs=[pl.BlockSpec(wqt_bf16.shape, lambda i: (0, 0)),
                  pl.BlockSpec(subkeys_bf16.shape, lambda i: (0, 0, 0, 0)),
                  pl.BlockSpec((tt, d), lambda i: (i, 0))],
        out_specs=[pl.BlockSpec((heads, PEER_TOPK, tt), lambda i: (0, 0, i)),
                   pl.BlockSpec((heads, PEER_TOPK, tt), lambda i: (0, 0, i))],
        out_shape=[jax.ShapeDtypeStruct((heads, PEER_TOPK, t), jnp.int32),
                   jax.ShapeDtypeStruct((heads, PEER_TOPK, t), F32)],
        compiler_params=_cparams("parallel"),
        name="peer_route",
    )(wqt_bf16, subkeys_bf16, x)


PEER_TOKEN_CHUNK = 1024
PEER_EXPERT_TILE = 256
PEER_PAIR_BLOCK = 1024
ROW_SUBLANES = 8


def _gelu(x):
    return 0.5 * x * (1.0 + lax.erf(x * (2.0 ** -0.5)))


def _peer_expert_kernel(meta_ref, ent_ref, gate_ref, x_ref, u_ref, v_ref, o_ref, stage, act):
    i = pl.program_id(0)
    first, lo, hi = meta_ref[2, i], meta_ref[3, i], meta_ref[4, i]

    @pl.when(first == 1)
    def _():
        o_ref[...] = jnp.zeros(o_ref.shape, F32)

    @pl.when(i == 0)
    def _():
        stage[...] = jnp.zeros(stage.shape, F32)

    @pl.when(hi > lo)
    def _():
        _peer_segment(ent_ref, gate_ref, x_ref, u_ref, v_ref, o_ref, stage, act, lo, hi)


def _peer_segment(ent_ref, gate_ref, x_ref, u_ref, v_ref, o_ref, stage, act, lo, hi):
    shift = PEER_EXPERT_TILE.bit_length() - 1
    lanes = stage.shape[-1]
    g_lo, g_hi = (lo + 7) // 8, hi // 8
    head_end = jnp.minimum(g_lo * 8, hi)
    tail_start = jnp.maximum(g_hi * 8, head_end)

    def unpack(j):
        ent = ent_ref[0, j]
        return lax.shift_right_logical(ent, shift), lax.bitwise_and(ent, PEER_EXPERT_TILE - 1)

    def sweep(one, group):
        lax.fori_loop(lo, head_end, lambda j, c: (one(j), c)[1], 0)
        lax.fori_loop(g_lo, g_hi, lambda s, c: (group(s), c)[1], 0)
        lax.fori_loop(tail_start, hi, lambda j, c: (one(j), c)[1], 0)

    def dot_pair(j, row):
        tok, e = unpack(j)
        pr = x_ref[tok] * u_ref[e]
        acc = pr[:, 0:lanes]
        for c in range(1, pr.shape[-1] // lanes):
            acc = acc + pr[:, c * lanes:(c + 1) * lanes]
        stage[pl.ds(row, 8), :] = acc

    def dot_group(s):
        base = pl.multiple_of(s * 64, 64)
        for r in range(8):
            dot_pair(s * 8 + r, base + r * 8)

    sweep(lambda j: dot_pair(j, pl.multiple_of(j * 8, 8)), dot_group)

    rr = lax.broadcasted_iota(jnp.int32, (128, 128), 0)
    cc = lax.broadcasted_iota(jnp.int32, (128, 128), 1)
    ones = jnp.ones((128, 128), BF16)

    def weights(g):
        base = pl.multiple_of(g * 1024, 1024)
        part = stage[pl.ds(base, 128, stride=8), :]
        for s in range(1, 8):
            part = part + stage[pl.ds(base + s, 128, stride=8), :]
        a = _gelu(_dot_exact_rhs(part, ones))
        gate = _dot_exact_rhs(jnp.where(rr == cc, gate_ref[pl.ds(g, 1), :], 0.0), ones)
        w = (a * gate).reshape(16, 8, 128)
        rows = pl.ds(pl.multiple_of(g * 16, 16), 16)
        for c in range(act.shape[-1] // 128):
            act[rows, :, c * 128:(c + 1) * 128] = w

    lax.fori_loop(lo // 256, (hi + 255) // 256, lambda t, c: (weights(2 * t), weights(2 * t + 1), c)[2], 0)

    def mix_one(j):
        tok, e = unpack(j)
        o_ref[tok] = o_ref[tok] + v_ref[e] * act[j // 8, pl.ds(j % 8, 1), :]

    def mix_group(s):
        toks, news = [], []
        acc = None
        for r in range(8):
            j = s * 8 + r
            tok, e = unpack(j)
            c = v_ref[e] * act[s, r:r + 1, :]
            acc = c if r == 0 else jnp.where(tok == toks[-1], acc, 0.0) + c
            toks.append(tok)
            news.append(o_ref[tok] + acc)
        for tok, new in zip(toks, news):
            o_ref[tok] = new

    sweep(mix_one, mix_group)


def _peer_plan(idx, gates, t_pad):
    heads, topk, t = idx.shape
    pb = PEER_PAIR_BLOCK
    n_tile = (PEER_NK * PEER_NK) // PEER_EXPERT_TILE
    n_cell = (t_pad // PEER_TOKEN_CHUNK) * n_tile
    n_pairs = heads * topk * t
    assert n_pairs % pb == 0
    n_blk = n_pairs // pb
    e = jnp.transpose(idx, (2, 0, 1)).reshape(-1)
    g = jnp.transpose(gates, (2, 0, 1)).reshape(-1)
    tok = jnp.repeat(jnp.arange(t, dtype=jnp.int32), heads * topk)
    cell = (tok // PEER_TOKEN_CHUNK) * n_tile + e // PEER_EXPERT_TILE
    ent = (tok % PEER_TOKEN_CHUNK) * PEER_EXPERT_TILE + e % PEER_EXPERT_TILE
    cell_s, ent_s, g_s = lax.sort((cell, ent, g), num_keys=1, is_stable=True)
    bounds = jnp.searchsorted(cell_s, jnp.arange(n_cell + 1, dtype=jnp.int32), side='left').astype(jnp.int32)
    first_blk = bounds[:-1] // pb
    n_seg_cell = jnp.where(bounds[1:] > bounds[:-1], (bounds[1:] - 1) // pb - first_blk + 1, 0)
    seg_end = jnp.cumsum(n_seg_cell)
    step = jnp.arange(n_blk + n_cell, dtype=jnp.int32)
    last = seg_end[-1] - 1
    step_c = jnp.minimum(step, last)
    q = jnp.minimum(jnp.searchsorted(seg_end, step_c, side='right'), n_cell - 1).astype(jnp.int32)
    blk = first_blk[q] + step_c - (seg_end - n_seg_cell)[q]
    lo = jnp.where(step <= last, jnp.clip(bounds[q] - blk * pb, 0, pb), 0)
    hi = jnp.where(step <= last, jnp.clip(bounds[q + 1] - blk * pb, 0, pb), 0)
    chunk = q // n_tile
    first = jnp.concatenate([jnp.ones((1,), jnp.int32), (chunk[1:] != chunk[:-1]).astype(jnp.int32)])
    meta = jnp.stack([chunk, q % n_tile, first, lo, hi, blk]).astype(jnp.int32)
    return meta, ent_s.reshape(n_blk, 1, pb), g_s.reshape(n_blk, pb // 128, 128)


def _peer_experts(x, idx, gates, u, v):
    t, d = x.shape
    t_pad = -(-t // PEER_TOKEN_CHUNK) * PEER_TOKEN_CHUNK
    w = d // ROW_SUBLANES
    meta, ent, gate = _peer_plan(idx, gates, t_pad)
    rows = lambda a: a.reshape(a.shape[0], ROW_SUBLANES, w)
    grid_spec = pltpu.PrefetchScalarGridSpec(
        num_scalar_prefetch=1,
        grid=(meta.shape[1],),
        in_specs=[pl.BlockSpec((None, 1, PEER_PAIR_BLOCK), lambda i, m: (m[5, i], 0, 0), memory_space=pltpu.SMEM),
                  pl.BlockSpec((None, PEER_PAIR_BLOCK // 128, 128), lambda i, m: (m[5, i], 0, 0)),
                  pl.BlockSpec((PEER_TOKEN_CHUNK, ROW_SUBLANES, w), lambda i, m: (m[0, i], 0, 0)),
                  pl.BlockSpec((PEER_EXPERT_TILE, ROW_SUBLANES, w), lambda i, m: (m[1, i], 0, 0)),
                  pl.BlockSpec((PEER_EXPERT_TILE, ROW_SUBLANES, w), lambda i, m: (m[1, i], 0, 0))],
        out_specs=pl.BlockSpec((PEER_TOKEN_CHUNK, ROW_SUBLANES, w), lambda i, m: (m[0, i], 0, 0)),
        scratch_shapes=[pltpu.VMEM((PEER_PAIR_BLOCK * 8, 128), F32),
                        pltpu.VMEM((PEER_PAIR_BLOCK // 8, 8, w), F32)])
    out = pl.pallas_call(
        _peer_expert_kernel,
        grid_spec=grid_spec,
        out_shape=jax.ShapeDtypeStruct((t_pad, ROW_SUBLANES, w), F32),
        compiler_params=pltpu.CompilerParams(dimension_semantics=("arbitrary",),
                                             vmem_limit_bytes=56 * 1024 * 1024),
        name="peer_experts",
    )(meta, ent, gate, rows(jnp.pad(x, ((0, t_pad - t), (0, 0)))), rows(u), rows(v))
    return out.reshape(t_pad, d)[:t]


def _peer_layer(x, wq, subkeys, u, v, g, b, alpha):
    idx, gates = _peer_route(x, wq.T.astype(BF16), subkeys.astype(BF16))
    return _res_ln(x, _peer_experts(x, idx, gates, u, v), g, b, alpha)


def _pad_cols(w, mult):
    return jnp.pad(w, ((0, 0), (0, -w.shape[1] % mult)))


def kernel(x_prompt, x_sample, cache_diff_k, cache_diff_v, state_mlstm_C, state_mlstm_n, state_mlstm_m, state_mlstm_conv, cache_sb_k, cache_sb_v, page_table, rel_bias, w_in_even, b_ig, b_fg, lam_q1, lam_k1, lam_q2, lam_k2, diff_norm_g, conv_w, conv_b, w_q_mlstm, w_k_mlstm, mlstm_norm_g, mlstm_skip, w_out_even, w_qkv_odd, w_out_odd, ln_g, ln_b, peer_wq, peer_subkeys, peer_u, peer_v):
    nb, seq, d = x_prompt.shape
    db, n_q, _ = x_sample.shape
    tp, ts = nb * seq, db * n_q
    n_pool = cache_diff_k.shape[1]
    depth = ln_g.shape[0]
    alpha = (2.0 * depth) ** 0.25
    h_a, h_b, h_c = d // 256, w_q_mlstm.shape[1], d // 128
    w_a, w_b = h_a * 2 * D_HA, d // 2
    assert depth == 2 and w_in_even.shape[0] == 1 and w_qkv_odd.shape[0] == 1 and w_a == w_b
    lam_init = 0.8 - 0.6 * math.exp(-0.3 * 0)
    x = jnp.concatenate([x_prompt.reshape(tp, d), x_sample.reshape(ts, d)])

    n_main = 3 * w_a + 3 * w_b
    proj = _matmul(x, _pad_cols(w_in_even[0], 896).astype(BF16), 896)
    qa, ka, va, u_b = (proj[:, i * w_a:(i + 1) * w_a] for i in range(4))
    gates = proj[:, n_main:n_main + 2 * h_b]
    lam_vec = jnp.stack([lam_q1[0], lam_k1[0], lam_q2[0], lam_k2[0]])
    ya_p = _diff_attention_prompt(proj, nb, seq, h_a, rel_bias, lam_vec, diff_norm_g[0], lam_init, 256)
    smp = lambda a: a[tp:].reshape(db, n_q, -1)
    ya_s = _diff_attention_sample(smp(qa), smp(ka), smp(va), cache_diff_k[0].reshape(n_pool, PAGE, w_a),
                                  cache_diff_v[0], page_table, rel_bias, lam_vec,
                                  diff_norm_g[0], lam_init)
    mp = {'b_ig': b_ig[0], 'b_fg': b_fg[0], 'conv_w': conv_w[0], 'conv_b': conv_b[0], 'w_q_mlstm': w_q_mlstm[0],
          'w_k_mlstm': w_k_mlstm[0], 'mlstm_norm_g': mlstm_norm_g[0], 'mlstm_skip': mlstm_skip[0]}
    lane0 = 3 * w_a // 256
    yb_p, c_p, n_p, m_p = _mlstm(proj, gates[:tp], nb, seq, MLSTM_CHUNK, h_b, lane0,
                                 jnp.zeros((nb, 8, w_b), F32), jnp.zeros((nb, h_b, 256, 256), F32),
                                 jnp.zeros((nb, h_b, 256), F32), jnp.zeros((nb, h_b), F32), mp)
    s_pad = 8
    pad_t = lambda a, val=0.0: jnp.pad(a, ((0, 0), (0, s_pad - n_q), (0, 0)), constant_values=val)
    gate_s = smp(gates)
    gate_s = jnp.concatenate([pad_t(gate_s[..., :h_b], NEG_BIG), pad_t(gate_s[..., h_b:], -NEG_BIG)], -1)
    conv0 = jnp.pad(state_mlstm_conv[0], ((0, 0), (8 - (CONV_W - 1), 0), (0, 0)))
    yb_s, c_s, n_s, m_s = _mlstm(pad_t(smp(proj)).reshape(db * s_pad, -1), gate_s.reshape(db * s_pad, -1), db, s_pad,
                                 s_pad, h_b, lane0, conv0, state_mlstm_C[0], state_mlstm_n[0], state_mlstm_m[0], mp)
    yb_s = yb_s.reshape(db, s_pad, w_b)[:, :n_q].reshape(ts, w_b)
    y = jnp.concatenate([jnp.concatenate([ya_p, yb_p], -1), jnp.concatenate([ya_s.reshape(ts, w_a), yb_s], -1)])
    x = _matmul_res_ln(y, w_out_even[0].astype(BF16), x, ln_g[0, 0], ln_b[0, 0], alpha)
    x = _peer_layer(x, peer_wq[0], peer_subkeys[0], peer_u[0], peer_v[0], ln_g[0, 1], ln_b[0, 1], alpha)

    qkv = _matmul(x, w_qkv_odd[0].astype(BF16), 1024)
    y_p = _sb_attention_prompt(qkv, nb, seq, h_c, 128, 256)
    q_s, k_s, v_s = (smp(qkv[:, i * d:(i + 1) * d]) for i in range(3))
    y_s = _sb_attention_sample(q_s, k_s, v_s, cache_sb_k[0], cache_sb_v[0], page_table)
    y = jnp.concatenate([y_p, y_s.reshape(ts, d)])
    x = _matmul_res_ln(y, w_out_odd[0].astype(BF16), x, ln_g[1, 0], ln_b[1, 0], alpha)
    x = _peer_layer(x, peer_wq[1], peer_subkeys[1], peer_u[1], peer_v[1], ln_g[1, 1], ln_b[1, 1], alpha)

    prm = lambda a, *s: a[:tp].reshape((1, nb, seq) + s)
    u_p = proj[:tp, 3 * w_a:3 * w_a + w_b].reshape(nb, seq, w_b)
    u_s = smp(u_b)
    sbk, sbv = qkv[:, d:2 * d], qkv[:, 2 * d:]
    return (x[:tp].reshape(nb, seq, d), x[tp:].reshape(db, n_q, d),
            prm(ka, h_a, 2, D_HA), prm(va, h_a, 2 * D_HA),
            smp(ka).reshape(1, db, n_q, h_a, 2, D_HA), smp(va).reshape(1, db, n_q, h_a, 2 * D_HA),
            c_p[None], n_p[None], m_p[None], u_p[None, :, seq - (CONV_W - 1):],
            c_s[None], n_s[None], m_s[None], u_s[None, :, n_q - (CONV_W - 1):],
            prm(sbk, h_c, 128), prm(sbv, h_c, 128),
            smp(sbk).reshape(1, db, n_q, h_c, 128), smp(sbv).reshape(1, db, n_q, h_c, 128))
```

```python
import functools
import math

import jax
import jax.numpy as jnp
import numpy as np
from jax import lax
from jax.experimental import pallas as pl
from jax.experimental.pallas import tpu as pltpu

F32 = jnp.float32
BF16 = jnp.bfloat16

LN_EPS = 1e-5
NEG_BIG = -1e30
VMEM_LIMIT = 48 * 1024 * 1024

D_HA = 64
N_BUCKETS = 32
MAX_DIST = 128
CONV_W = 4
MLSTM_CHUNK = 128
PAGE = 128
PEER_HEADS = 8
PEER_NK = 128
PEER_TOPK = 16


def _cparams(*sem):
    return pltpu.CompilerParams(dimension_semantics=sem, vmem_limit_bytes=VMEM_LIMIT)


def _dot(a, b):
    return jnp.dot(a, b, preferred_element_type=F32)


def _dot_nt(a, b):
    return lax.dot_general(a, b, (((1,), (1,)), ((), ())), preferred_element_type=F32)


def _dot_tn(a, b):
    return lax.dot_general(a, b, (((0,), (0,)), ((), ())), preferred_element_type=F32)


def _split3(x):
    h1 = x.astype(BF16)
    r1 = x - h1.astype(F32)
    h2 = r1.astype(BF16)
    h3 = (r1 - h2.astype(F32)).astype(BF16)
    return h1, h2, h3


def _dot_exact_rhs(x, m_bf16):
    h1, h2, h3 = _split3(x)
    return _dot(h1, m_bf16) + _dot(h2, m_bf16) + _dot(h3, m_bf16)


def _row_tile(m):
    for t in (384, 256, 128, 64, 32, 16, 8):
        if m % t == 0:
            return t
    raise ValueError(f"row count {m} is not a multiple of 8")


def _mm_kernel(x_ref, w_ref, o_ref):
    o_ref[...] = _dot(x_ref[...].astype(BF16), w_ref[...])


def _matmul(x, w_bf16, tn):
    m, k = x.shape
    n = w_bf16.shape[1]
    tm = _row_tile(m)
    return pl.pallas_call(
        _mm_kernel,
        grid=(m // tm, n // tn),
        in_specs=[pl.BlockSpec((tm, k), lambda i, j: (i, 0)),
                  pl.BlockSpec((k, tn), lambda i, j: (0, j))],
        out_specs=pl.BlockSpec((tm, tn), lambda i, j: (i, j)),
        out_shape=jax.ShapeDtypeStruct((m, n), F32),
        compiler_params=_cparams("parallel", "arbitrary"),
        name="matmul",
    )(x, w_bf16)


def _mm_nt_kernel(w_ref, x_ref, o_ref):
    o_ref[...] = _dot_nt(w_ref[...], x_ref[...].astype(BF16))


def _matmul_t(wt_bf16, x):
    n, k = wt_bf16.shape
    m = x.shape[0]
    tm = _row_tile(m)
    return pl.pallas_call(
        _mm_nt_kernel,
        grid=(m // tm,),
        in_specs=[pl.BlockSpec((n, k), lambda i: (0, 0)),
                  pl.BlockSpec((tm, k), lambda i: (i, 0))],
        out_specs=pl.BlockSpec((n, tm), lambda i: (0, i)),
        out_shape=jax.ShapeDtypeStruct((n, m), F32),
        compiler_params=_cparams("parallel"),
        name="matmul_t",
    )(wt_bf16, x)


def _layer_norm(xf, g, b):
    mu = jnp.mean(xf, -1, keepdims=True)
    xc = xf - mu
    var = jnp.mean(xc * xc, -1, keepdims=True)
    return xc * lax.rsqrt(var + LN_EPS) * g + b


def _mm_res_ln_kernel(alpha, a_ref, w_ref, x_ref, g_ref, b_ref, o_ref):
    y = _dot(a_ref[...].astype(BF16), w_ref[...])
    o_ref[...] = _layer_norm(alpha * x_ref[...] + y, g_ref[...], b_ref[...])


def _matmul_res_ln(a, w_bf16, x, g, b, alpha):
    m, k = a.shape
    n = w_bf16.shape[1]
    tm = _row_tile(m)
    return pl.pallas_call(
        functools.partial(_mm_res_ln_kernel, alpha),
        grid=(m // tm,),
        in_specs=[pl.BlockSpec((tm, k), lambda i: (i, 0)),
                  pl.BlockSpec((k, n), lambda i: (0, 0)),
                  pl.BlockSpec((tm, n), lambda i: (i, 0)),
                  pl.BlockSpec((1, n), lambda i: (0, 0)),
                  pl.BlockSpec((1, n), lambda i: (0, 0))],
        out_specs=pl.BlockSpec((tm, n), lambda i: (i, 0)),
        out_shape=jax.ShapeDtypeStruct((m, n), F32),
        compiler_params=_cparams("parallel"),
        name="matmul_res_ln",
    )(a, w_bf16, x, g.reshape(1, n), b.reshape(1, n))


def _res_ln_kernel(alpha, x_ref, f_ref, g_ref, b_ref, o_ref):
    o_ref[...] = _layer_norm(alpha * x_ref[...] + f_ref[...], g_ref[...], b_ref[...])


def _res_ln(x, f, g, b, alpha):
    m, n = x.shape
    tm = _row_tile(m)
    return pl.pallas_call(
        functools.partial(_res_ln_kernel, alpha),
        grid=(m // tm,),
        in_specs=[pl.BlockSpec((tm, n), lambda i: (i, 0)),
                  pl.BlockSpec((tm, n), lambda i: (i, 0)),
                  pl.BlockSpec((1, n), lambda i: (0, 0)),
                  pl.BlockSpec((1, n), lambda i: (0, 0))],
        out_specs=pl.BlockSpec((tm, n), lambda i: (i, 0)),
        out_shape=jax.ShapeDtypeStruct((m, n), F32),
        compiler_params=_cparams("parallel"),
        name="res_ln",
    )(x, f, g.reshape(1, n), b.reshape(1, n))


def _t5_bucket(dist):
    n = jnp.maximum(dist, 0)
    exact = N_BUCKETS // 2
    large = exact + (jnp.log(jnp.maximum(n, exact).astype(F32) / exact)
                     / math.log(MAX_DIST / exact) * (N_BUCKETS - exact)).astype(jnp.int32)
    return jnp.where(n < exact, n, jnp.minimum(large, N_BUCKETS - 1))


def _lambda(lam_ref, lam_init):
    lq1, lk1, lq2, lk2 = lam_ref[0:1, :], lam_ref[1:2, :], lam_ref[2:3, :], lam_ref[3:4, :]
    return (jnp.exp(jnp.sum(lq1 * lk1, keepdims=True)) - jnp.exp(jnp.sum(lq2 * lk2, keepdims=True))
            + lam_init)


def _diff_finish(acc1, l1, acc2, l2, lam, g, lam_init):
    o = acc1 / l1 - lam * (acc2 / l2)
    ms = jnp.mean(o * o, -1, keepdims=True)
    return o * lax.rsqrt(ms + LN_EPS) * g * (1.0 - lam_init)


def _diff_prompt_kernel(tq, lam_init, q_ref, k_ref, v_ref, bias_ref, far_ref, lam_ref, g_ref, o_ref):
    qi = pl.program_id(2)
    scale = D_HA ** -0.5
    q = q_ref[...]
    lane = lax.broadcasted_iota(jnp.int32, q.shape, 1)
    q1 = jnp.where(lane < D_HA, q, 0.0).astype(BF16)
    q2 = jnp.where(lane >= D_HA, q, 0.0).astype(BF16)
    dv = v_ref.shape[-1]

    def update(state, ki, bias, causal):
        m1, l1, a1, m2, l2, a2 = state
        start = pl.multiple_of(ki * tq, tq)
        kb = k_ref[pl.ds(start, tq), :].astype(BF16)
        vb = v_ref[pl.ds(start, tq), :].astype(BF16)
        out = []
        for qm, m, l, a in ((q1, m1, l1, a1), (q2, m2, l2, a2)):
            s = _dot_nt(qm, kb) * scale + bias
            if causal:
                r = lax.broadcasted_iota(jnp.int32, s.shape, 0)
                c = lax.broadcasted_iota(jnp.int32, s.shape, 1)
                s = jnp.where(c <= r, s, NEG_BIG)
            mn = jnp.maximum(m, jnp.max(s, -1, keepdims=True))
            p = jnp.exp(s - mn)
            alpha = jnp.exp(m - mn)
            out += [mn, alpha * l + jnp.sum(p, -1, keepdims=True), alpha * a + _dot(p.astype(BF16), vb)]
        return tuple(out)

    init = (jnp.full((tq, 1), NEG_BIG, F32), jnp.zeros((tq, 1), F32), jnp.zeros((tq, dv), F32)) * 2
    state = update(init, qi, bias_ref[0], True)
    sub = jnp.where(qi >= 1, bias_ref[1], NEG_BIG)
    state = update(state, jnp.maximum(qi - 1, 0), sub, False)
    far = far_ref[...]
    m1, l1, a1, m2, l2, a2 = lax.fori_loop(0, jnp.maximum(qi - 1, 0),
                                           lambda ki, st: update(st, ki, far, False), state)
    o_ref[...] = _diff_finish(a1, l1, a2, l2, _lambda(lam_ref, lam_init), g_ref[...], lam_init)


def _diff_bias_tables(rel_bias, tq):
    h = rel_bias.shape[1]
    r = jnp.arange(tq, dtype=jnp.int32)[:, None]
    c = jnp.arange(tq, dtype=jnp.int32)[None, :]
    tiles = jnp.stack([rel_bias[_t5_bucket(r - c)], rel_bias[_t5_bucket(r - c + tq)]])
    return jnp.transpose(tiles, (3, 0, 1, 2)), rel_bias[N_BUCKETS - 1].reshape(h, 1, 1)


def _diff_attention_prompt(proj, batch, seq, n_head, rel_bias, lam_vec, g, lam_init, tq):
    assert tq >= MAX_DIST and seq % tq == 0
    dh = 2 * D_HA
    nq = seq // tq
    bias_tiles, far = _diff_bias_tables(rel_bias, tq)
    return pl.pallas_call(
        functools.partial(_diff_prompt_kernel, tq, lam_init),
        grid=(batch, n_head, nq),
        in_specs=[pl.BlockSpec((tq, dh), lambda b, h, i: (b * nq + i, h)),
                  pl.BlockSpec((seq, dh), lambda b, h, i: (b, n_head + h)),
                  pl.BlockSpec((seq, dh), lambda b, h, i: (b, 2 * n_head + h)),
                  pl.BlockSpec((None, 2, tq, tq), lambda b, h, i: (h, 0, 0, 0)),
                  pl.BlockSpec((None, 1, 1), lambda b, h, i: (h, 0, 0)),
                  pl.BlockSpec((4, D_HA), lambda b, h, i: (0, 0)),
                  pl.BlockSpec((1, dh), lambda b, h, i: (0, 0))],
        out_specs=pl.BlockSpec((tq, dh), lambda b, h, i: (b * nq + i, h)),
        out_shape=jax.ShapeDtypeStruct((batch * seq, n_head * dh), F32),
        compiler_params=_cparams("parallel", "parallel", "arbitrary"),
        name="diff_attn_prompt",
    )(proj, proj, proj, bias_tiles, far, lam_vec, g.reshape(1, dh))


def _log_sigmoid_pair(z):
    t = jnp.log(1.0 + jnp.exp(-jnp.abs(z)))
    return jnp.minimum(z, 0.0) - t, jnp.minimum(-z, 0.0) - t


def _suffix_matrix(n):
    j = lax.broadcasted_iota(jnp.int32, (n, n), 0)
    s = lax.broadcasted_iota(jnp.int32, (n, n), 1)
    return jnp.where(j > s, 1.0, 0.0).astype(BF16)


def _sb_prompt_kernel(tq, scale, q_ref, k_ref, v_ref, o_ref):
    qi = pl.program_id(2)
    q = q_ref[...].astype(BF16)
    dv = v_ref.shape[-1]
    suffix = _suffix_matrix(tq)

    def update(state, ki, causal):
        run, acc = state
        start = pl.multiple_of(ki * tq, tq)
        kb = k_ref[pl.ds(start, tq), :].astype(BF16)
        vb = v_ref[pl.ds(start, tq), :].astype(BF16)
        z = _dot_nt(q, kb) * scale
        ls, lk = _log_sigmoid_pair(z)
        if causal:
            r = lax.broadcasted_iota(jnp.int32, z.shape, 0)
            c = lax.broadcasted_iota(jnp.int32, z.shape, 1)
            mask = c < r
            lk = jnp.where(mask, lk, 0.0)
        after = _dot_exact_rhs(lk, suffix) + run
        a = jnp.exp(ls + after)
        if causal:
            a = jnp.where(mask, a, 0.0)
        return run + jnp.sum(lk, -1, keepdims=True), acc + _dot(a.astype(BF16), vb)

    state = update((jnp.zeros((tq, 1), F32), jnp.zeros((tq, dv), F32)), qi, True)
    _, acc = lax.fori_loop(0, qi, lambda j, st: update(st, qi - 1 - j, False), state)
    o_ref[...] = acc


def _sb_attention_prompt(qkv, batch, seq, n_head, dh, tq):
    nq = seq // tq
    return pl.pallas_call(
        functools.partial(_sb_prompt_kernel, tq, dh ** -0.5),
        grid=(batch, n_head, nq),
        in_specs=[pl.BlockSpec((tq, dh), lambda b, h, i: (b * nq + i, h)),
                  pl.BlockSpec((seq, dh), lambda b, h, i: (b, n_head + h)),
                  pl.BlockSpec((seq, dh), lambda b, h, i: (b, 2 * n_head + h))],
        out_specs=pl.BlockSpec((tq, dh), lambda b, h, i: (b * nq + i, h)),
        out_shape=jax.ShapeDtypeStruct((batch * seq, n_head * dh), F32),
        compiler_params=_cparams("parallel", "parallel", "arbitrary"),
        name="sb_attn_prompt",
    )(qkv, qkv, qkv)


def _mlstm_kernel(L, u_ref, v_ref, ob_ref, grow_ref, bcol_ref, conv0_ref, cw_ref, cb_ref,
                  wq_ref, wk_ref, c0_ref, n0_ref, m0_ref, g_ref, skip_ref,
                  y_ref, c_out, n_out, m_out, ext, c_s, n_s, m_s):
    c = pl.program_id(2)
    dk = u_ref.shape[-1]

    @pl.when(c == 0)
    def _():
        ext[0:8, :] = conv0_ref[...]
        c_s[...] = c0_ref[...]
        n_s[...] = n0_ref[...]
        m_s[...] = m0_ref[...]

    ext[8:8 + L, :] = u_ref[...]
    uc = cb_ref[...]
    for j in range(CONV_W):
        uc = uc + ext[pl.ds(8 - (CONV_W - 1) + j, L), :] * cw_ref[j:j + 1, :]
    ext[0:8, :] = ext[L:L + 8, :]
    ua = uc * jax.nn.sigmoid(uc)
    uab = ua.astype(BF16)
    qf = _dot(uab, wq_ref[...])
    q = qf.astype(BF16)
    kf = _dot(uab, wk_ref[...]) * dk ** -0.5
    k = kf.astype(BF16)
    vb = v_ref[...].astype(BF16)

    gr = grow_ref[...] + bcol_ref[...]
    li_r, lf_r = gr[0:1, :], _log_sigmoid_pair(gr[1:2, :])[0]
    t = lax.broadcasted_iota(jnp.int32, (L, L), 0)
    s = lax.broadcasted_iota(jnp.int32, (L, L), 1)
    causal = s <= t
    li_c = jnp.sum(jnp.where(s == t, li_r, 0.0), -1, keepdims=True)
    b_c = _dot_exact_rhs(jnp.where(causal, lf_r, 0.0), jnp.ones((L, 8), BF16))[:, 0:1]
    b_r = _dot_exact_rhs(jnp.broadcast_to(lf_r, (8, L)), jnp.where(t <= s, 1.0, 0.0).astype(BF16))[0:1, :]

    m = m_s[:, 0:1]
    cmat = c_s[...]
    nrow = n_s[...]
    d = jnp.where(causal, b_c - b_r + li_r, NEG_BIG)
    inter = b_c + m
    mt = jnp.maximum(inter, jnp.max(d, -1, keepdims=True))
    w = jnp.exp(d - mt) * _dot_nt(q, k)
    sc = jnp.exp(inter - mt)
    num = _dot(w.astype(BF16), vb) + sc * _dot(q, cmat.astype(BF16))
    den = jnp.sum(w, -1, keepdims=True) + sc * jnp.sum(qf * nrow, -1, keepdims=True)
    h = num / jnp.maximum(jnp.abs(den), jnp.exp(-mt))

    m_new = mt[L - 1:L, :]
    b_last = b_c[L - 1:L, :]
    wl = jnp.exp(b_last - b_c + li_c - m_new)
    dec = jnp.exp(b_last + m - m_new)
    kw = kf * wl
    c_s[...] = dec * cmat + _dot_tn(kw.astype(BF16), vb)
    n_s[...] = dec * nrow + jnp.sum(kw, 0, keepdims=True)
    m_s[...] = jnp.broadcast_to(m_new, m_s.shape)

    mu = jnp.mean(h, -1, keepdims=True)
    hc = h - mu
    var = jnp.mean(hc * hc, -1, keepdims=True)
    hn = hc * lax.rsqrt(var + LN_EPS) * g_ref[...]
    y_ref[...] = jax.nn.sigmoid(ob_ref[...]) * (hn + skip_ref[...] * ua)

    @pl.when(c == pl.num_programs(2) - 1)
    def _():
        c_out[...] = c_s[...]
        n_out[...] = n_s[...]
        m_out[...] = m_s[...]


def _mlstm(rows, gates, batch, seq, L, n_head, lane0, conv0, c0, n0, m0, p):
    dk = 256
    nc = seq // L
    t_rows = batch * seq
    grow = jnp.transpose(gates.reshape(batch * nc, L, 2, n_head), (3, 0, 2, 1))
    bias = jnp.stack([p['b_ig'], p['b_fg']], -1)
    outs = pl.pallas_call(
        functools.partial(_mlstm_kernel, L),
        grid=(batch, n_head, nc),
        in_specs=[pl.BlockSpec((L, dk), lambda b, h, c: (b * nc + c, lane0 + h)),
                  pl.BlockSpec((L, dk), lambda b, h, c: (b * nc + c, lane0 + n_head + h)),
                  pl.BlockSpec((L, dk), lambda b, h, c: (b * nc + c, lane0 + 2 * n_head + h)),
                  pl.BlockSpec((None, None, 2, L), lambda b, h, c: (h, b * nc + c, 0, 0)),
                  pl.BlockSpec((None, 2, 1), lambda b, h, c: (h, 0, 0)),
                  pl.BlockSpec((None, 8, dk), lambda b, h, c: (b, 0, h)),
                  pl.BlockSpec((CONV_W, dk), lambda b, h, c: (0, h)),
                  pl.BlockSpec((1, dk), lambda b, h, c: (0, h)),
                  pl.BlockSpec((None, dk, dk), lambda b, h, c: (h, 0, 0)),
                  pl.BlockSpec((None, dk, dk), lambda b, h, c: (h, 0, 0)),
                  pl.BlockSpec((None, None, dk, dk), lambda b, h, c: (b, h, 0, 0)),
                  pl.BlockSpec((None, None, 1, dk), lambda b, h, c: (b, h, 0, 0)),
                  pl.BlockSpec((None, None, 1, 128), lambda b, h, c: (b, h, 0, 0)),
                  pl.BlockSpec((1, dk), lambda b, h, c: (0, h)),
                  pl.BlockSpec((1, dk), lambda b, h, c: (0, h))],
        out_specs=[pl.BlockSpec((L, dk), lambda b, h, c: (b * nc + c, h)),
                   pl.BlockSpec((None, None, dk, dk), lambda b, h, c: (b, h, 0, 0)),
                   pl.BlockSpec((None, None, 1, dk), lambda b, h, c: (b, h, 0, 0)),
                   pl.BlockSpec((None, None, 1, 128), lambda b, h, c: (b, h, 0, 0))],
        out_shape=[jax.ShapeDtypeStruct((t_rows, n_head * dk), F32),
                   jax.ShapeDtypeStruct((batch, n_head, dk, dk), F32),
                   jax.ShapeDtypeStruct((batch, n_head, 1, dk), F32),
                   jax.ShapeDtypeStruct((batch, n_head, 1, 128), F32)],
        scratch_shapes=[pltpu.VMEM((L + 8, dk), F32), pltpu.VMEM((dk, dk), F32),
                        pltpu.VMEM((1, dk), F32), pltpu.VMEM((1, 128), F32)],
        compiler_params=_cparams("parallel", "parallel", "arbitrary"),
        name="mlstm",
    )(rows, rows, rows, grow, bias.reshape(n_head, 2, 1), conv0,
      p['conv_w'], p['conv_b'].reshape(1, -1), p['w_q_mlstm'].astype(BF16), p['w_k_mlstm'].astype(BF16),
      c0, n0.reshape(batch, n_head, 1, dk), jnp.broadcast_to(m0[:, :, None, None], (batch, n_head, 1, 128)),
      p['mlstm_norm_g'].reshape(1, -1), p['mlstm_skip'].reshape(1, -1))
    y, c_new, n_new, m_new = outs
    return y, c_new, n_new.reshape(batch, n_head, dk), m_new[:, :, 0, 0]


def _segment_matrix(n_seg, width, order):
    rows = np.zeros((n_seg, n_seg * width), np.float32)
    for i in range(n_seg):
        rows[order[i], i * width:(i + 1) * width] = 1.0
    return jnp.asarray(rows, BF16)


def _segment_scores(seg, kq):
    h1, h2, _ = _split3(kq)
    return _dot_nt(seg, h1) + _dot_nt(seg, h2)


def _key_head_matrices(n_key, n_head, rows):
    col = np.arange(n_key * n_head)
    spread = (col[None, :] // n_head == np.arange(n_key)[:, None]).astype(np.float32)
    own = (col[None, :] % n_head == (np.arange(rows) % n_head)[:, None]).astype(np.float32)
    return jnp.asarray(spread, BF16), jnp.asarray(spread.T, BF16), jnp.asarray(own)


def _spread_dot(a, own, spread, v2):
    return _dot((_dot(a.astype(BF16), spread) * own).astype(BF16), v2)


def _diff_sample_kernel(n_q, n_page, lam_init, pt_ref, q_ref, kc_ref, vc_ref, kn_ref, vn_ref, bias_ref, seg_ref,
                        own_ref, spread_ref, lam_ref, g_ref, o_ref, m_s, l_s, a_s):
    p = pl.program_id(1)
    n_head = seg_ref.shape[0] // 2
    scale = D_HA ** -0.5

    @pl.when(p == 0)
    def _():
        m_s[...] = jnp.full(m_s.shape, NEG_BIG, F32)
        l_s[...] = jnp.zeros(l_s.shape, F32)
        a_s[...] = jnp.zeros(a_s.shape, F32)

    def block(k_ref, v_ref):
        k = k_ref[...]
        sq = [_segment_scores(seg_ref[...], k * q_ref[i:i + 1, :]) for i in range(n_q)]
        v2 = v_ref[...].reshape(-1, v_ref.shape[-1]).astype(BF16)
        for mi in range(2):
            s = jnp.concatenate([x[mi * n_head:(mi + 1) * n_head] for x in sq], 0) * scale + bias_ref[...]
            m = m_s[mi]
            mn = jnp.maximum(m, jnp.max(s, -1, keepdims=True))
            pr = jnp.exp(s - mn)
            alpha = jnp.exp(m - mn)
            m_s[mi] = mn
            l_s[mi] = alpha * l_s[mi] + jnp.sum(pr, -1, keepdims=True)
            a_s[mi] = alpha * a_s[mi] + _spread_dot(pr, own_ref[...], spread_ref[...], v2)

    @pl.when(p < n_page)
    def _():
        block(kc_ref, vc_ref)

    @pl.when(p == n_page)
    def _():
        block(kn_ref, vn_ref)
        o_ref[...] = _diff_finish(a_s[0], l_s[0], a_s[1], l_s[1], _lambda(lam_ref, lam_init), g_ref[...], lam_init)


def _diff_attention_sample(q, k_new, v_new, cache_k, cache_v, page_table, rel_bias, lam_vec, g, lam_init):
    db, n_q, width = q.shape
    n_head = width // (2 * D_HA)
    dv = cache_v.shape[-1]
    n_page = page_table.shape[1]
    past = n_page * PAGE
    pad_new = lambda a: jnp.pad(a, ((0, 0), (0, PAGE - n_q), (0, 0)))
    qpos = past + jnp.arange(n_q, dtype=jnp.int32)
    kpos = jnp.arange(past + PAGE, dtype=jnp.int32)
    bias = rel_bias[_t5_bucket(qpos[:, None] - kpos[None, :])]
    bias = jnp.where((kpos[None, :] <= qpos[:, None])[..., None], bias, NEG_BIG)
    bias = jnp.transpose(bias, (0, 2, 1)).reshape(n_q * n_head, n_page + 1, PAGE)
    bias = jnp.transpose(bias, (1, 0, 2))
    seg = _segment_matrix(2 * n_head, D_HA, [(i % 2) * n_head + i // 2 for i in range(2 * n_head)])
    rows = n_q * n_head
    spread, _, own = _key_head_matrices(PAGE, n_head, rows)
    page = lambda b, p, pt: (pt[b, jnp.minimum(p, n_page - 1)], 0, 0)
    page4 = lambda b, p, pt: (pt[b, jnp.minimum(p, n_page - 1)], 0, 0, 0)
    const = lambda b, p, pt: (0, 0)
    grid_spec = pltpu.PrefetchScalarGridSpec(
        num_scalar_prefetch=1,
        grid=(db, n_page + 1),
        in_specs=[pl.BlockSpec((None, n_q, width), lambda b, p, pt: (b, 0, 0)),
                  pl.BlockSpec((None, PAGE, width), page),
                  pl.BlockSpec((None, PAGE, n_head, dv), page4),
                  pl.BlockSpec((None, PAGE, width), lambda b, p, pt: (b, 0, 0)),
                  pl.BlockSpec((None, PAGE, n_head, dv), lambda b, p, pt: (b, 0, 0, 0)),
                  pl.BlockSpec((None, rows, PAGE), lambda b, p, pt: (p, 0, 0)),
                  pl.BlockSpec(seg.shape, const),
                  pl.BlockSpec(own.shape, const),
                  pl.BlockSpec(spread.shape, const),
                  pl.BlockSpec((4, D_HA), const),
                  pl.BlockSpec((1, dv), const)],
        out_specs=pl.BlockSpec((None, rows, dv), lambda b, p, pt: (b, 0, 0)),
        scratch_shapes=[pltpu.VMEM((2, rows, 1), F32), pltpu.VMEM((2, rows, 1), F32),
                        pltpu.VMEM((2, rows, dv), F32)])
    out = pl.pallas_call(
        functools.partial(_diff_sample_kernel, n_q, n_page, lam_init),
        grid_spec=grid_spec,
        out_shape=jax.ShapeDtypeStruct((db, rows, dv), F32),
        compiler_params=_cparams("parallel", "arbitrary"),
        name="diff_attn_sample",
    )(page_table, q, cache_k, cache_v, pad_new(k_new), pad_new(v_new).reshape(db, PAGE, n_head, dv), bias, seg,
      own, spread, lam_vec, g.reshape(1, -1))
    return out.reshape(db, n_q, width)


def _sb_sample_kernel(scale, pt_ref, q_ref, kc_ref, vc_ref, kn_ref, vn_ref, mask_ref, own_ref, spread_ref,
                      collapse_ref, o_ref, run_s, a_s):
    p = pl.program_id(1)
    keys, _, dh = kc_ref.shape
    suffix = _suffix_matrix(keys)
    q = q_ref[...].astype(BF16)

    def block(k_ref, v_ref, mask):
        k2 = k_ref[...].reshape(-1, dh).astype(BF16)
        v2 = v_ref[...].reshape(-1, dh).astype(BF16)
        own = own_ref[...]
        z = _dot_exact_rhs(_dot_nt(q, k2) * own, collapse_ref[...]) * scale
        ls, lk = _log_sigmoid_pair(z)
        if mask is not None:
            lk = lk * mask
        run = run_s[...]
        a = jnp.exp(ls + _dot_exact_rhs(lk, suffix) + run)
        if mask is not None:
            a = a * mask
        run_s[...] = run + jnp.sum(lk, -1, keepdims=True)
        a_s[...] += _spread_dot(a, own, spread_ref[...], v2)

    @pl.when(p == 0)
    def _():
        run_s[...] = jnp.zeros(run_s.shape, F32)
        a_s[...] = jnp.zeros(a_s.shape, F32)
        block(kn_ref, vn_ref, mask_ref[...])

    @pl.when(p > 0)
    def _():
        block(kc_ref, vc_ref, None)

    @pl.when(p == pl.num_programs(1) - 1)
    def _():
        o_ref[...] = a_s[...]


def _sb_attention_sample(q, k_new, v_new, cache_k, cache_v, page_table):
    db, n_q, width = q.shape
    n_head, dh = cache_k.shape[2:]
    n_page = page_table.shape[1]
    pad_new = lambda a: jnp.pad(a, ((0, 0), (0, PAGE - n_q), (0, 0))).reshape(db, PAGE, n_head, dh)
    qi = jnp.repeat(jnp.arange(n_q, dtype=jnp.int32), n_head)[:, None]
    mask = (jnp.arange(PAGE, dtype=jnp.int32)[None, :] < qi).astype(F32)
    rows = n_q * n_head
    spread, collapse, own = _key_head_matrices(PAGE, n_head, rows)
    page = lambda b, p, pt: (pt[b, n_page - jnp.maximum(p, 1)], 0, 0, 0)
    const = lambda b, p, pt: (0, 0)
    grid_spec = pltpu.PrefetchScalarGridSpec(
        num_scalar_prefetch=1,
        grid=(db, n_page + 1),
        in_specs=[pl.BlockSpec((None, rows, dh), lambda b, p, pt: (b, 0, 0)),
                  pl.BlockSpec((None, PAGE, n_head, dh), page),
                  pl.BlockSpec((None, PAGE, n_head, dh), page),
                  pl.BlockSpec((None, PAGE, n_head, dh), lambda b, p, pt: (b, 0, 0, 0)),
                  pl.BlockSpec((None, PAGE, n_head, dh), lambda b, p, pt: (b, 0, 0, 0)),
                  pl.BlockSpec(mask.shape, const),
                  pl.BlockSpec(own.shape, const),
                  pl.BlockSpec(spread.shape, const),
                  pl.BlockSpec(collapse.shape, const)],
        out_specs=pl.BlockSpec((None, rows, dh), lambda b, p, pt: (b, 0, 0)),
        scratch_shapes=[pltpu.VMEM((rows, 1), F32), pltpu.VMEM((rows, dh), F32)])
    out = pl.pallas_call(
        functools.partial(_sb_sample_kernel, dh ** -0.5),
        grid_spec=grid_spec,
        out_shape=jax.ShapeDtypeStruct((db, rows, dh), F32),
        compiler_params=_cparams("parallel", "arbitrary"),
        name="sb_attn_sample",
    )(page_table, q.reshape(db, rows, dh), cache_k, cache_v, pad_new(k_new), pad_new(v_new), mask, own, spread,
      collapse)
    return out.reshape(db, n_q, width)


def _top_k_rows(arrays, k):
    n, lanes = arrays[0].shape
    row = lax.broadcasted_iota(jnp.int32, (n, lanes), 0).astype(F32)
    slot = lax.broadcasted_iota(jnp.int32, (k, lanes), 0)

    def body(i, carry):
        out = []
        for s, vals, ids in carry:
            m = jnp.max(s, 0, keepdims=True)
            first = jnp.min(jnp.where(s == m, row, float(n)), 0, keepdims=True)
            out.append((jnp.where(row == first, -jnp.inf, s), jnp.where(slot == i, m, vals),
                        jnp.where(slot == i, first, ids)))
        return tuple(out)

    zero = jnp.zeros((k, lanes), F32)
    done = lax.fori_loop(0, k, body, tuple((s, zero, zero) for s in arrays))
    return [(vals, ids) for _, vals, ids in done]


def _outer_rows(a, b, fn):
    return jnp.concatenate([fn(a[i:i + 1, :], b) for i in range(a.shape[0])], 0)


def _peer_route_kernel(wqt_ref, keys_ref, x_ref, idx_ref, g_ref):
    nk = keys_ref.shape[2]
    qt = _dot_nt(wqt_ref[...], x_ref[...].astype(BF16)).astype(BF16)
    half = keys_ref.shape[3]
    for h in range(keys_ref.shape[0]):
        scores = [_dot(keys_ref[h, p], qt[(h * 2 + p) * half:(h * 2 + p + 1) * half, :]) for p in range(2)]
        (s1, i1), (s2, i2) = _top_k_rows(scores, PEER_TOPK)
        cand = _outer_rows(s1, s2, lambda a, b: a + b)
        expert = _outer_rows(i1, i2, lambda a, b: a * nk + b)
        (sc, ci), = _top_k_rows([cand], PEER_TOPK)
        crow = lax.broadcasted_iota(jnp.int32, cand.shape, 0).astype(F32)
        picked = [jnp.sum(jnp.where(crow == ci[j:j + 1, :], expert, 0.0), 0, keepdims=True) for j in range(PEER_TOPK)]
        e = jnp.exp(sc - jnp.max(sc, 0, keepdims=True))
        idx_ref[h] = jnp.concatenate(picked, 0).astype(jnp.int32)
        g_ref[h] = e / jnp.sum(e, 0, keepdims=True)


def _peer_route(x, wqt_bf16, subkeys_bf16):
    t, d = x.shape
    heads = subkeys_bf16.shape[0]
    tt = 128
    return pl.pallas_call(
        _peer_route_kernel,
        grid=(t // tt,),
        in_specs=[pl.BlockSpec(wqt_bf16.shape, lambda i: (0, 0)),
                  pl.BlockSpec(subkeys_bf16.shape, lambda i: (0, 0, 0, 0)),
                  pl.BlockSpec((tt, d), lambda i: (i, 0))],
        out_specs=[pl.BlockSpec((heads, PEER_TOPK, tt), lambda i: (0, 0, i)),
                   pl.BlockSpec((heads, PEER_TOPK, tt), lambda i: (0, 0, i))],
        out_shape=[jax.ShapeDtypeStruct((heads, PEER_TOPK, t), jnp.int32),
                   jax.ShapeDtypeStruct((heads, PEER_TOPK, t), F32)],
        compiler_params=_cparams("parallel"),
        name="peer_route",
    )(wqt_bf16, subkeys_bf16, x)


PEER_TOKEN_CHUNK = 1024
PEER_EXPERT_TILE = 256
PEER_PAIR_BLOCK = 1024


def _gelu(x):
    return 0.5 * x * (1.0 + lax.erf(x * (2.0 ** -0.5)))


def _peer_expert_kernel(rpt, meta_ref, trow_ref, erow_ref, gate_ref, x_ref, u_ref, v_ref, o_ref, stage, act):
    i = pl.program_id(0)
    first, lo, hi = meta_ref[2, i], meta_ref[3, i], meta_ref[4, i]

    @pl.when(first == 1)
    def _():
        o_ref[...] = jnp.zeros(o_ref.shape, F32)

    @pl.when(i == 0)
    def _():
        stage[...] = jnp.zeros(stage.shape, F32)

    @pl.when(hi > lo)
    def _():
        _peer_segment(rpt, trow_ref, erow_ref, gate_ref, x_ref, u_ref, v_ref, o_ref, stage, act, lo, hi)


def _peer_segment(rpt, trow_ref, erow_ref, gate_ref, x_ref, u_ref, v_ref, o_ref, stage, act, lo, hi):
    g_lo, g_hi = (lo + 7) // 8, hi // 8
    head_end = jnp.minimum(g_lo * 8, hi)
    tail_start = jnp.maximum(g_hi * 8, head_end)

    def rows(ref, j):
        return pl.ds(pl.multiple_of(ref[0, j], rpt), rpt)

    def sweep(one, group):
        lax.fori_loop(lo, head_end, lambda j, c: (one(j), c)[1], 0)
        lax.fori_loop(g_lo, g_hi, lambda s, c: (group(s), c)[1], 0)
        lax.fori_loop(tail_start, hi, lambda j, c: (one(j), c)[1], 0)

    def dot_pair(j, row):
        pr = x_ref[rows(trow_ref, j), :] * u_ref[rows(erow_ref, j), :]
        acc = pr[0:8]
        for c in range(1, rpt // 8):
            acc = acc + pr[c * 8:(c + 1) * 8]
        stage[pl.ds(row, 8), :] = acc

    def dot_group(s):
        base = pl.multiple_of(s * 64, 64)
        for r in range(8):
            dot_pair(s * 8 + r, base + r * 8)

    sweep(lambda j: dot_pair(j, pl.multiple_of(j * 8, 8)), dot_group)

    rr = lax.broadcasted_iota(jnp.int32, (128, 128), 0)
    cc = lax.broadcasted_iota(jnp.int32, (128, 128), 1)
    ones = jnp.ones((128, 128), BF16)

    def weights(g):
        base = pl.multiple_of(g * 1024, 1024)
        part = stage[pl.ds(base, 128, stride=8), :]
        for s in range(1, 8):
            part = part + stage[pl.ds(base + s, 128, stride=8), :]
        a = _gelu(_dot_exact_rhs(part, ones))
        gate = _dot_exact_rhs(jnp.where(rr == cc, gate_ref[pl.ds(g, 1), :], 0.0), ones)
        act[pl.ds(pl.multiple_of(g * 16, 16), 16)] = (a * gate).reshape(16, 8, 128)

    lax.fori_loop(lo // 256, (hi + 255) // 256, lambda t, c: (weights(2 * t), weights(2 * t + 1), c)[2], 0)

    def mix_one(j):
        out = rows(trow_ref, j)
        o_ref[out, :] = o_ref[out, :] + v_ref[rows(erow_ref, j), :] * act[j // 8, pl.ds(j % 8, 1), :]

    def mix_group(s):
        toks, news = [], []
        acc = None
        for r in range(8):
            j = s * 8 + r
            tok = trow_ref[0, j]
            c = v_ref[rows(erow_ref, j), :] * act[s, r:r + 1, :]
            acc = c if r == 0 else jnp.where(tok == toks[-1], acc, 0.0) + c
            toks.append(tok)
            news.append(o_ref[pl.ds(pl.multiple_of(tok, rpt), rpt), :] + acc)
        for tok, new in zip(toks, news):
            o_ref[pl.ds(pl.multiple_of(tok, rpt), rpt), :] = new

    sweep(mix_one, mix_group)


def _peer_plan(idx, gates, t_pad, rpt):
    heads, topk, t = idx.shape
    pb = PEER_PAIR_BLOCK
    e_bits = PEER_EXPERT_TILE.bit_length() - 1
    c_bits = e_bits + PEER_TOKEN_CHUNK.bit_length() - 1
    n_tile = (PEER_NK * PEER_NK) // PEER_EXPERT_TILE
    n_cell = (t_pad // PEER_TOKEN_CHUNK) * n_tile
    n_pairs = heads * topk * t
    assert n_pairs % pb == 0 and (n_cell << c_bits) < 2 ** 31
    n_blk = n_pairs // pb
    e = jnp.transpose(idx, (2, 0, 1)).reshape(-1)
    g = jnp.transpose(gates, (2, 0, 1)).reshape(-1)
    tok = jnp.repeat(jnp.arange(t, dtype=jnp.int32), heads * topk)
    cell = (tok // PEER_TOKEN_CHUNK) * n_tile + e // PEER_EXPERT_TILE
    key = (cell << c_bits) | ((tok % PEER_TOKEN_CHUNK) << e_bits) | (e % PEER_EXPERT_TILE)
    key_s, g_s = lax.sort((key, g), num_keys=1, is_stable=False)
    bounds = jnp.searchsorted(key_s, jnp.arange(n_cell + 1, dtype=jnp.int32) << c_bits, side='left').astype(jnp.int32)
    first_blk = bounds[:-1] // pb
    n_seg_cell = jnp.where(bounds[1:] > bounds[:-1], (bounds[1:] - 1) // pb - first_blk + 1, 0)
    seg_end = jnp.cumsum(n_seg_cell)
    step = jnp.arange(n_blk + n_cell, dtype=jnp.int32)
    last = seg_end[-1] - 1
    step_c = jnp.minimum(step, last)
    q = jnp.minimum(jnp.searchsorted(seg_end, step_c, side='right'), n_cell - 1).astype(jnp.int32)
    blk = first_blk[q] + step_c - (seg_end - n_seg_cell)[q]
    lo = jnp.where(step <= last, jnp.clip(bounds[q] - blk * pb, 0, pb), 0)
    hi = jnp.where(step <= last, jnp.clip(bounds[q + 1] - blk * pb, 0, pb), 0)
    chunk = q // n_tile
    first = jnp.concatenate([jnp.ones((1,), jnp.int32), (chunk[1:] != chunk[:-1]).astype(jnp.int32)])
    meta = jnp.stack([chunk, q % n_tile, first, lo, hi, blk]).astype(jnp.int32)
    trow = ((key_s >> e_bits) & (PEER_TOKEN_CHUNK - 1)) * rpt
    erow = (key_s & (PEER_EXPERT_TILE - 1)) * rpt
    return meta, trow.reshape(n_blk, 1, pb), erow.reshape(n_blk, 1, pb), g_s.reshape(n_blk, pb // 128, 128)


def _peer_experts(x, idx, gates, u, v):
    t, d = x.shape
    t_pad = -(-t // PEER_TOKEN_CHUNK) * PEER_TOKEN_CHUNK
    rpt = d // 128
    meta, trow, erow, gate = _peer_plan(idx, gates, t_pad, rpt)
    rows = lambda a: a.reshape(a.shape[0] * rpt, 128)
    pairs = pl.BlockSpec((None, 1, PEER_PAIR_BLOCK), lambda i, m: (m[5, i], 0, 0), memory_space=pltpu.SMEM)
    tokens = pl.BlockSpec((PEER_TOKEN_CHUNK * rpt, 128), lambda i, m: (m[0, i], 0))
    experts = pl.BlockSpec((PEER_EXPERT_TILE * rpt, 128), lambda i, m: (m[1, i], 0))
    grid_spec = pltpu.PrefetchScalarGridSpec(
        num_scalar_prefetch=1,
        grid=(meta.shape[1],),
        in_specs=[pairs, pairs,
                  pl.BlockSpec((None, PEER_PAIR_BLOCK // 128, 128), lambda i, m: (m[5, i], 0, 0)),
                  tokens, experts, experts],
        out_specs=tokens,
        scratch_shapes=[pltpu.VMEM((PEER_PAIR_BLOCK * 8, 128), F32),
                        pltpu.VMEM((PEER_PAIR_BLOCK // 8, 8, 128), F32)])
    out = pl.pallas_call(
        functools.partial(_peer_expert_kernel, rpt),
        grid_spec=grid_spec,
        out_shape=jax.ShapeDtypeStruct((t_pad * rpt, 128), F32),
        compiler_params=pltpu.CompilerParams(dimension_semantics=("arbitrary",),
                                             vmem_limit_bytes=56 * 1024 * 1024),
        name="peer_experts",
    )(meta, trow, erow, gate, rows(jnp.pad(x, ((0, t_pad - t), (0, 0)))), rows(u), rows(v))
    return out.reshape(t_pad, d)[:t]


def _peer_layer(x, wq, subkeys, u, v, g, b, alpha):
    idx, gates = _peer_route(x, wq.T.astype(BF16), subkeys.astype(BF16))
    return _res_ln(x, _peer_experts(x, idx, gates, u, v), g, b, alpha)


def _pad_cols(w, mult):
    return jnp.pad(w, ((0, 0), (0, -w.shape[1] % mult)))


def kernel(x_prompt, x_sample, cache_diff_k, cache_diff_v, state_mlstm_C, state_mlstm_n, state_mlstm_m, state_mlstm_conv, cache_sb_k, cache_sb_v, page_table, rel_bias, w_in_even, b_ig, b_fg, lam_q1, lam_k1, lam_q2, lam_k2, diff_norm_g, conv_w, conv_b, w_q_mlstm, w_k_mlstm, mlstm_norm_g, mlstm_skip, w_out_even, w_qkv_odd, w_out_odd, ln_g, ln_b, peer_wq, peer_subkeys, peer_u, peer_v):
    nb, seq, d = x_prompt.shape
    db, n_q, _ = x_sample.shape
    tp, ts = nb * seq, db * n_q
    n_pool = cache_diff_k.shape[1]
    depth = ln_g.shape[0]
    alpha = (2.0 * depth) ** 0.25
    h_a, h_b, h_c = d // 256, w_q_mlstm.shape[1], d // 128
    w_a, w_b = h_a * 2 * D_HA, d // 2
    assert depth == 2 and w_in_even.shape[0] == 1 and w_qkv_odd.shape[0] == 1 and w_a == w_b
    lam_init = 0.8 - 0.6 * math.exp(-0.3 * 0)
    x = jnp.concatenate([x_prompt.reshape(tp, d), x_sample.reshape(ts, d)])

    n_main = 3 * w_a + 3 * w_b
    proj = _matmul(x, _pad_cols(w_in_even[0], 896).astype(BF16), 896)
    qa, ka, va, u_b = (proj[:, i * w_a:(i + 1) * w_a] for i in range(4))
    gates = proj[:, n_main:n_main + 2 * h_b]
    lam_vec = jnp.stack([lam_q1[0], lam_k1[0], lam_q2[0], lam_k2[0]])
    ya_p = _diff_attention_prompt(proj, nb, seq, h_a, rel_bias, lam_vec, diff_norm_g[0], lam_init, 256)
    smp = lambda a: a[tp:].reshape(db, n_q, -1)
    ya_s = _diff_attention_sample(smp(qa), smp(ka), smp(va), cache_diff_k[0].reshape(n_pool, PAGE, w_a),
                                  cache_diff_v[0], page_table, rel_bias, lam_vec,
                                  diff_norm_g[0], lam_init)
    mp = {'b_ig': b_ig[0], 'b_fg': b_fg[0], 'conv_w': conv_w[0], 'conv_b': conv_b[0], 'w_q_mlstm': w_q_mlstm[0],
          'w_k_mlstm': w_k_mlstm[0], 'mlstm_norm_g': mlstm_norm_g[0], 'mlstm_skip': mlstm_skip[0]}
    lane0 = 3 * w_a // 256
    yb_p, c_p, n_p, m_p = _mlstm(proj, gates[:tp], nb, seq, MLSTM_CHUNK, h_b, lane0,
                                 jnp.zeros((nb, 8, w_b), F32), jnp.zeros((nb, h_b, 256, 256), F32),
                                 jnp.zeros((nb, h_b, 256), F32), jnp.zeros((nb, h_b), F32), mp)
    s_pad = 8
    pad_t = lambda a, val=0.0: jnp.pad(a, ((0, 0), (0, s_pad - n_q), (0, 0)), constant_values=val)
    gate_s = smp(gates)
    gate_s = jnp.concatenate([pad_t(gate_s[..., :h_b], NEG_BIG), pad_t(gate_s[..., h_b:], -NEG_BIG)], -1)
    conv0 = jnp.pad(state_mlstm_conv[0], ((0, 0), (8 - (CONV_W - 1), 0), (0, 0)))
    yb_s, c_s, n_s, m_s = _mlstm(pad_t(smp(proj)).reshape(db * s_pad, -1), gate_s.reshape(db * s_pad, -1), db, s_pad,
                                 s_pad, h_b, lane0, conv0, state_mlstm_C[0], state_mlstm_n[0], state_mlstm_m[0], mp)
    yb_s = yb_s.reshape(db, s_pad, w_b)[:, :n_q].reshape(ts, w_b)
    y = jnp.concatenate([jnp.concatenate([ya_p, yb_p], -1), jnp.concatenate([ya_s.reshape(ts, w_a), yb_s], -1)])
    x = _matmul_res_ln(y, w_out_even[0].astype(BF16), x, ln_g[0, 0], ln_b[0, 0], alpha)
    x = _peer_layer(x, peer_wq[0], peer_subkeys[0], peer_u[0], peer_v[0], ln_g[0, 1], ln_b[0, 1], alpha)

    qkv = _matmul(x, w_qkv_odd[0].astype(BF16), 1024)
    y_p = _sb_attention_prompt(qkv, nb, seq, h_c, 128, 256)
    q_s, k_s, v_s = (smp(qkv[:, i * d:(i + 1) * d]) for i in range(3))
    y_s = _sb_attention_sample(q_s, k_s, v_s, cache_sb_k[0], cache_sb_v[0], page_table)
    y = jnp.concatenate([y_p, y_s.reshape(ts, d)])
    x = _matmul_res_ln(y, w_out_odd[0].astype(BF16), x, ln_g[1, 0], ln_b[1, 0], alpha)
    x = _peer_layer(x, peer_wq[1], peer_subkeys[1], peer_u[1], peer_v[1], ln_g[1, 1], ln_b[1, 1], alpha)

    prm = lambda a, *s: a[:tp].reshape((1, nb, seq) + s)
    u_p = proj[:tp, 3 * w_a:3 * w_a + w_b].reshape(nb, seq, w_b)
    u_s = smp(u_b)
    sbk, sbv = qkv[:, d:2 * d], qkv[:, 2 * d:]
    return (x[:tp].reshape(nb, seq, d), x[tp:].reshape(db, n_q, d),
            prm(ka, h_a, 2, D_HA), prm(va, h_a, 2 * D_HA),
            smp(ka).reshape(1, db, n_q, h_a, 2, D_HA), smp(va).reshape(1, db, n_q, h_a, 2 * D_HA),
            c_p[None], n_p[None], m_p[None], u_p[None, :, seq - (CONV_W - 1):],
            c_s[None], n_s[None], m_s[None], u_s[None, :, n_q - (CONV_W - 1):],
            prm(sbk, h_c, 128), prm(sbv, h_c, 128),
            smp(sbk).reshape(1, db, n_q, h_c, 128), smp(sbv).reshape(1, db, n_q, h_c, 128))
```

```python
import functools
import math

import jax
import jax.numpy as jnp
import numpy as np
from jax import lax
from jax.experimental import pallas as pl
from jax.experimental.pallas import tpu as pltpu

F32 = jnp.float32
BF16 = jnp.bfloat16

LN_EPS = 1e-5
NEG_BIG = -1e30
VMEM_LIMIT = 48 * 1024 * 1024

D_HA = 64
N_BUCKETS = 32
MAX_DIST = 128
CONV_W = 4
MLSTM_CHUNK = 128
PAGE = 128
PEER_HEADS = 8
PEER_NK = 128
PEER_TOPK = 16


def _cparams(*sem):
    return pltpu.CompilerParams(dimension_semantics=sem, vmem_limit_bytes=VMEM_LIMIT)


def _dot(a, b):
    return jnp.dot(a, b, preferred_element_type=F32)


def _dot_nt(a, b):
    return lax.dot_general(a, b, (((1,), (1,)), ((), ())), preferred_element_type=F32)


def _dot_tn(a, b):
    return lax.dot_general(a, b, (((0,), (0,)), ((), ())), preferred_element_type=F32)


def _split3(x):
    h1 = x.astype(BF16)
    r1 = x - h1.astype(F32)
    h2 = r1.astype(BF16)
    h3 = (r1 - h2.astype(F32)).astype(BF16)
    return h1, h2, h3


def _dot_exact_rhs(x, m_bf16, parts=3):
    return sum(_dot(h, m_bf16) for h in _split3(x)[:parts])


def _row_tile(m):
    for t in (384, 256, 128, 64, 32, 16, 8):
        if m % t == 0:
            return t
    raise ValueError(f"row count {m} is not a multiple of 8")


def _mm_kernel(x_ref, w_ref, o_ref):
    o_ref[...] = _dot(x_ref[...].astype(BF16), w_ref[...])


def _matmul(x, w_bf16, tn):
    m, k = x.shape
    n = w_bf16.shape[1]
    tm = _row_tile(m)
    return pl.pallas_call(
        _mm_kernel,
        grid=(m // tm, n // tn),
        in_specs=[pl.BlockSpec((tm, k), lambda i, j: (i, 0)),
                  pl.BlockSpec((k, tn), lambda i, j: (0, j))],
        out_specs=pl.BlockSpec((tm, tn), lambda i, j: (i, j)),
        out_shape=jax.ShapeDtypeStruct((m, n), F32),
        compiler_params=_cparams("parallel", "arbitrary"),
        name="matmul",
    )(x, w_bf16)


def _mm_nt_kernel(w_ref, x_ref, o_ref):
    o_ref[...] = _dot_nt(w_ref[...], x_ref[...].astype(BF16))


def _matmul_t(wt_bf16, x):
    n, k = wt_bf16.shape
    m = x.shape[0]
    tm = _row_tile(m)
    return pl.pallas_call(
        _mm_nt_kernel,
        grid=(m // tm,),
        in_specs=[pl.BlockSpec((n, k), lambda i: (0, 0)),
                  pl.BlockSpec((tm, k), lambda i: (i, 0))],
        out_specs=pl.BlockSpec((n, tm), lambda i: (0, i)),
        out_shape=jax.ShapeDtypeStruct((n, m), F32),
        compiler_params=_cparams("parallel"),
        name="matmul_t",
    )(wt_bf16, x)


def _layer_norm(xf, g, b):
    mu = jnp.mean(xf, -1, keepdims=True)
    xc = xf - mu
    var = jnp.mean(xc * xc, -1, keepdims=True)
    return xc * lax.rsqrt(var + LN_EPS) * g + b


def _mm_res_ln_kernel(alpha, a_ref, w_ref, x_ref, g_ref, b_ref, o_ref):
    y = _dot(a_ref[...].astype(BF16), w_ref[...])
    o_ref[...] = _layer_norm(alpha * x_ref[...] + y, g_ref[...], b_ref[...])


def _matmul_res_ln(a, w_bf16, x, g, b, alpha):
    m, k = a.shape
    n = w_bf16.shape[1]
    tm = _row_tile(m)
    return pl.pallas_call(
        functools.partial(_mm_res_ln_kernel, alpha),
        grid=(m // tm,),
        in_specs=[pl.BlockSpec((tm, k), lambda i: (i, 0)),
                  pl.BlockSpec((k, n), lambda i: (0, 0)),
                  pl.BlockSpec((tm, n), lambda i: (i, 0)),
                  pl.BlockSpec((1, n), lambda i: (0, 0)),
                  pl.BlockSpec((1, n), lambda i: (0, 0))],
        out_specs=pl.BlockSpec((tm, n), lambda i: (i, 0)),
        out_shape=jax.ShapeDtypeStruct((m, n), F32),
        compiler_params=_cparams("parallel"),
        name="matmul_res_ln",
    )(a, w_bf16, x, g.reshape(1, n), b.reshape(1, n))


def _res_ln_kernel(alpha, x_ref, f_ref, g_ref, b_ref, o_ref):
    o_ref[...] = _layer_norm(alpha * x_ref[...] + f_ref[...], g_ref[...], b_ref[...])


def _res_ln(x, f, g, b, alpha):
    m, n = x.shape
    tm = _row_tile(m)
    return pl.pallas_call(
        functools.partial(_res_ln_kernel, alpha),
        grid=(m // tm,),
        in_specs=[pl.BlockSpec((tm, n), lambda i: (i, 0)),
                  pl.BlockSpec((tm, n), lambda i: (i, 0)),
                  pl.BlockSpec((1, n), lambda i: (0, 0)),
                  pl.BlockSpec((1, n), lambda i: (0, 0))],
        out_specs=pl.BlockSpec((tm, n), lambda i: (i, 0)),
        out_shape=jax.ShapeDtypeStruct((m, n), F32),
        compiler_params=_cparams("parallel"),
        name="res_ln",
    )(x, f, g.reshape(1, n), b.reshape(1, n))


def _t5_bucket(dist):
    n = jnp.maximum(dist, 0)
    exact = N_BUCKETS // 2
    large = exact + (jnp.log(jnp.maximum(n, exact).astype(F32) / exact)
                     / math.log(MAX_DIST / exact) * (N_BUCKETS - exact)).astype(jnp.int32)
    return jnp.where(n < exact, n, jnp.minimum(large, N_BUCKETS - 1))


def _lambda(lam_ref, lam_init):
    lq1, lk1, lq2, lk2 = lam_ref[0:1, :], lam_ref[1:2, :], lam_ref[2:3, :], lam_ref[3:4, :]
    return (jnp.exp(jnp.sum(lq1 * lk1, keepdims=True)) - jnp.exp(jnp.sum(lq2 * lk2, keepdims=True))
            + lam_init)


def _diff_finish(acc1, l1, acc2, l2, lam, g, lam_init):
    o = acc1 / l1 - lam * (acc2 / l2)
    ms = jnp.mean(o * o, -1, keepdims=True)
    return o * lax.rsqrt(ms + LN_EPS) * g * (1.0 - lam_init)


def _diff_prompt_kernel(tq, lam_init, q_ref, k_ref, v_ref, bias_ref, far_ref, lam_ref, g_ref, o_ref):
    qi = pl.program_id(2)
    scale = D_HA ** -0.5
    q = q_ref[...]
    lane = lax.broadcasted_iota(jnp.int32, q.shape, 1)
    q1 = jnp.where(lane < D_HA, q, 0.0).astype(BF16)
    q2 = jnp.where(lane >= D_HA, q, 0.0).astype(BF16)
    dv = v_ref.shape[-1]

    def update(state, ki, bias, causal):
        m1, l1, a1, m2, l2, a2 = state
        start = pl.multiple_of(ki * tq, tq)
        kb = k_ref[pl.ds(start, tq), :].astype(BF16)
        vb = v_ref[pl.ds(start, tq), :].astype(BF16)
        out = []
        for qm, m, l, a in ((q1, m1, l1, a1), (q2, m2, l2, a2)):
            s = _dot_nt(qm, kb) * scale + bias
            if causal:
                r = lax.broadcasted_iota(jnp.int32, s.shape, 0)
                c = lax.broadcasted_iota(jnp.int32, s.shape, 1)
                s = jnp.where(c <= r, s, NEG_BIG)
            mn = jnp.maximum(m, jnp.max(s, -1, keepdims=True))
            p = jnp.exp(s - mn)
            alpha = jnp.exp(m - mn)
            out += [mn, alpha * l + jnp.sum(p, -1, keepdims=True), alpha * a + _dot(p.astype(BF16), vb)]
        return tuple(out)

    init = (jnp.full((tq, 1), NEG_BIG, F32), jnp.zeros((tq, 1), F32), jnp.zeros((tq, dv), F32)) * 2
    state = update(init, qi, bias_ref[0], True)
    sub = jnp.where(qi >= 1, bias_ref[1], NEG_BIG)
    state = update(state, jnp.maximum(qi - 1, 0), sub, False)
    far = far_ref[...]
    n_far = jnp.maximum(qi - 1, 0)
    state = lax.fori_loop(0, n_far // 2,
                          lambda j, st: update(update(st, 2 * j, far, False), 2 * j + 1, far, False), state)
    m1, l1, a1, m2, l2, a2 = lax.fori_loop(0, n_far % 2, lambda j, st: update(st, n_far - 1, far, False), state)
    o_ref[...] = _diff_finish(a1, l1, a2, l2, _lambda(lam_ref, lam_init), g_ref[...], lam_init)


def _diff_bias_tables(rel_bias, tq):
    h = rel_bias.shape[1]
    r = jnp.arange(tq, dtype=jnp.int32)[:, None]
    c = jnp.arange(tq, dtype=jnp.int32)[None, :]
    tiles = jnp.stack([rel_bias[_t5_bucket(r - c)], rel_bias[_t5_bucket(r - c + tq)]])
    return jnp.transpose(tiles, (3, 0, 1, 2)), rel_bias[N_BUCKETS - 1].reshape(h, 1, 1)


def _diff_attention_prompt(proj, batch, seq, n_head, rel_bias, lam_vec, g, lam_init, tq):
    assert tq >= MAX_DIST and seq % tq == 0
    dh = 2 * D_HA
    nq = seq // tq
    bias_tiles, far = _diff_bias_tables(rel_bias, tq)
    return pl.pallas_call(
        functools.partial(_diff_prompt_kernel, tq, lam_init),
        grid=(batch, n_head, nq),
        in_specs=[pl.BlockSpec((tq, dh), lambda b, h, i: (b * nq + i, h)),
                  pl.BlockSpec((seq, dh), lambda b, h, i: (b, n_head + h)),
                  pl.BlockSpec((seq, dh), lambda b, h, i: (b, 2 * n_head + h)),
                  pl.BlockSpec((None, 2, tq, tq), lambda b, h, i: (h, 0, 0, 0)),
                  pl.BlockSpec((None, 1, 1), lambda b, h, i: (h, 0, 0)),
                  pl.BlockSpec((4, D_HA), lambda b, h, i: (0, 0)),
                  pl.BlockSpec((1, dh), lambda b, h, i: (0, 0))],
        out_specs=pl.BlockSpec((tq, dh), lambda b, h, i: (b * nq + i, h)),
        out_shape=jax.ShapeDtypeStruct((batch * seq, n_head * dh), F32),
        compiler_params=_cparams("parallel", "parallel", "arbitrary"),
        name="diff_attn_prompt",
    )(proj, proj, proj, bias_tiles, far, lam_vec, g.reshape(1, dh))


def _log_sigmoid_pair(z):
    t = jnp.log(1.0 + jnp.exp(-jnp.abs(z)))
    return jnp.minimum(z, 0.0) - t, jnp.minimum(-z, 0.0) - t


def _suffix_matrix(n):
    j = lax.broadcasted_iota(jnp.int32, (n, n), 0)
    s = lax.broadcasted_iota(jnp.int32, (n, n), 1)
    return jnp.where(j > s, 1.0, 0.0).astype(BF16)


def _sb_prompt_kernel(tq, scale, q_ref, k_ref, v_ref, o_ref):
    qi = pl.program_id(2)
    q = q_ref[...].astype(BF16)
    dv = v_ref.shape[-1]
    suffix = _suffix_matrix(tq)

    def update(state, ki, causal):
        run, acc = state
        start = pl.multiple_of(ki * tq, tq)
        kb = k_ref[pl.ds(start, tq), :].astype(BF16)
        vb = v_ref[pl.ds(start, tq), :].astype(BF16)
        z = _dot_nt(q, kb) * scale
        ls, lk = _log_sigmoid_pair(z)
        if causal:
            r = lax.broadcasted_iota(jnp.int32, z.shape, 0)
            c = lax.broadcasted_iota(jnp.int32, z.shape, 1)
            mask = c < r
            lk = jnp.where(mask, lk, 0.0)
        after = _dot_exact_rhs(lk, suffix, parts=2) + run
        a = jnp.exp(ls + after)
        if causal:
            a = jnp.where(mask, a, 0.0)
        return run + jnp.sum(lk, -1, keepdims=True), acc + _dot(a.astype(BF16), vb)

    state = update((jnp.zeros((tq, 1), F32), jnp.zeros((tq, dv), F32)), qi, True)
    state = lax.fori_loop(0, qi // 2,
                          lambda j, st: update(update(st, qi - 1 - 2 * j, False), qi - 2 - 2 * j, False), state)
    _, acc = lax.fori_loop(0, qi % 2, lambda j, st: update(st, 0, False), state)
    o_ref[...] = acc


def _sb_attention_prompt(qkv, batch, seq, n_head, dh, tq):
    nq = seq // tq
    return pl.pallas_call(
        functools.partial(_sb_prompt_kernel, tq, dh ** -0.5),
        grid=(batch, n_head, nq),
        in_specs=[pl.BlockSpec((tq, dh), lambda b, h, i: (b * nq + i, h)),
                  pl.BlockSpec((seq, dh), lambda b, h, i: (b, n_head + h)),
                  pl.BlockSpec((seq, dh), lambda b, h, i: (b, 2 * n_head + h))],
        out_specs=pl.BlockSpec((tq, dh), lambda b, h, i: (b * nq + i, h)),
        out_shape=jax.ShapeDtypeStruct((batch * seq, n_head * dh), F32),
        compiler_params=_cparams("parallel", "parallel", "arbitrary"),
        name="sb_attn_prompt",
    )(qkv, qkv, qkv)


def _mlstm_kernel(L, u_ref, v_ref, ob_ref, grow_ref, bcol_ref, conv0_ref, cw_ref, cb_ref,
                  wq_ref, wk_ref, c0_ref, n0_ref, m0_ref, g_ref, skip_ref,
                  y_ref, c_out, n_out, m_out, ext, c_s, n_s, m_s):
    c = pl.program_id(2)
    dk = u_ref.shape[-1]

    @pl.when(c == 0)
    def _():
        ext[0:8, :] = conv0_ref[...]
        c_s[...] = c0_ref[...]
        n_s[...] = n0_ref[...]
        m_s[...] = m0_ref[...]

    ext[8:8 + L, :] = u_ref[...]
    uc = cb_ref[...]
    for j in range(CONV_W):
        uc = uc + ext[pl.ds(8 - (CONV_W - 1) + j, L), :] * cw_ref[j:j + 1, :]
    ext[0:8, :] = ext[L:L + 8, :]
    ua = uc * jax.nn.sigmoid(uc)
    uab = ua.astype(BF16)
    qf = _dot(uab, wq_ref[...])
    q = qf.astype(BF16)
    kf = _dot(uab, wk_ref[...]) * dk ** -0.5
    k = kf.astype(BF16)
    vb = v_ref[...].astype(BF16)

    gr = grow_ref[...] + bcol_ref[...]
    li_r, lf_r = gr[0:1, :], _log_sigmoid_pair(gr[1:2, :])[0]
    t = lax.broadcasted_iota(jnp.int32, (L, L), 0)
    s = lax.broadcasted_iota(jnp.int32, (L, L), 1)
    causal = s <= t
    li_c = jnp.sum(jnp.where(s == t, li_r, 0.0), -1, keepdims=True)
    b_c = _dot_exact_rhs(jnp.where(causal, lf_r, 0.0), jnp.ones((L, 8), BF16))[:, 0:1]
    b_r = _dot_exact_rhs(jnp.broadcast_to(lf_r, (8, L)), jnp.where(t <= s, 1.0, 0.0).astype(BF16))[0:1, :]

    m = m_s[:, 0:1]
    cmat = c_s[...]
    nrow = n_s[...]
    d = jnp.where(causal, b_c - b_r + li_r, NEG_BIG)
    inter = b_c + m
    mt = jnp.maximum(inter, jnp.max(d, -1, keepdims=True))
    w = jnp.exp(d - mt) * _dot_nt(q, k)
    sc = jnp.exp(inter - mt)
    num = _dot(w.astype(BF16), vb) + sc * _dot(q, cmat.astype(BF16))
    den = jnp.sum(w, -1, keepdims=True) + sc * jnp.sum(qf * nrow, -1, keepdims=True)
    h = num / jnp.maximum(jnp.abs(den), jnp.exp(-mt))

    m_new = mt[L - 1:L, :]
    b_last = b_c[L - 1:L, :]
    wl = jnp.exp(b_last - b_c + li_c - m_new)
    dec = jnp.exp(b_last + m - m_new)
    kw = kf * wl
    c_s[...] = dec * cmat + _dot_tn(kw.astype(BF16), vb)
    n_s[...] = dec * nrow + jnp.sum(kw, 0, keepdims=True)
    m_s[...] = jnp.broadcast_to(m_new, m_s.shape)

    mu = jnp.mean(h, -1, keepdims=True)
    hc = h - mu
    var = jnp.mean(hc * hc, -1, keepdims=True)
    hn = hc * lax.rsqrt(var + LN_EPS) * g_ref[...]
    y_ref[...] = jax.nn.sigmoid(ob_ref[...]) * (hn + skip_ref[...] * ua)

    @pl.when(c == pl.num_programs(2) - 1)
    def _():
        c_out[...] = c_s[...]
        n_out[...] = n_s[...]
        m_out[...] = m_s[...]


def _mlstm(rows, gates, batch, seq, L, n_head, lane0, conv0, c0, n0, m0, p):
    dk = 256
    nc = seq // L
    t_rows = batch * seq
    grow = jnp.transpose(gates.reshape(batch * nc, L, 2, n_head), (3, 0, 2, 1))
    bias = jnp.stack([p['b_ig'], p['b_fg']], -1)
    outs = pl.pallas_call(
        functools.partial(_mlstm_kernel, L),
        grid=(batch, n_head, nc),
        in_specs=[pl.BlockSpec((L, dk), lambda b, h, c: (b * nc + c, lane0 + h)),
                  pl.BlockSpec((L, dk), lambda b, h, c: (b * nc + c, lane0 + n_head + h)),
                  pl.BlockSpec((L, dk), lambda b, h, c: (b * nc + c, lane0 + 2 * n_head + h)),
                  pl.BlockSpec((None, None, 2, L), lambda b, h, c: (h, b * nc + c, 0, 0)),
                  pl.BlockSpec((None, 2, 1), lambda b, h, c: (h, 0, 0)),
                  pl.BlockSpec((None, 8, dk), lambda b, h, c: (b, 0, h)),
                  pl.BlockSpec((CONV_W, dk), lambda b, h, c: (0, h)),
                  pl.BlockSpec((1, dk), lambda b, h, c: (0, h)),
                  pl.BlockSpec((None, dk, dk), lambda b, h, c: (h, 0, 0)),
                  pl.BlockSpec((None, dk, dk), lambda b, h, c: (h, 0, 0)),
                  pl.BlockSpec((None, None, dk, dk), lambda b, h, c: (b, h, 0, 0)),
                  pl.BlockSpec((None, None, 1, dk), lambda b, h, c: (b, h, 0, 0)),
                  pl.BlockSpec((None, None, 1, 128), lambda b, h, c: (b, h, 0, 0)),
                  pl.BlockSpec((1, dk), lambda b, h, c: (0, h)),
                  pl.BlockSpec((1, dk), lambda b, h, c: (0, h))],
        out_specs=[pl.BlockSpec((L, dk), lambda b, h, c: (b * nc + c, h)),
                   pl.BlockSpec((None, None, dk, dk), lambda b, h, c: (b, h, 0, 0)),
                   pl.BlockSpec((None, None, 1, dk), lambda b, h, c: (b, h, 0, 0)),
                   pl.BlockSpec((None, None, 1, 128), lambda b, h, c: (b, h, 0, 0))],
        out_shape=[jax.ShapeDtypeStruct((t_rows, n_head * dk), F32),
                   jax.ShapeDtypeStruct((batch, n_head, dk, dk), F32),
                   jax.ShapeDtypeStruct((batch, n_head, 1, dk), F32),
                   jax.ShapeDtypeStruct((batch, n_head, 1, 128), F32)],
        scratch_shapes=[pltpu.VMEM((L + 8, dk), F32), pltpu.VMEM((dk, dk), F32),
                        pltpu.VMEM((1, dk), F32), pltpu.VMEM((1, 128), F32)],
        compiler_params=_cparams("parallel", "parallel", "arbitrary"),
        name="mlstm",
    )(rows, rows, rows, grow, bias.reshape(n_head, 2, 1), conv0,
      p['conv_w'], p['conv_b'].reshape(1, -1), p['w_q_mlstm'].astype(BF16), p['w_k_mlstm'].astype(BF16),
      c0, n0.reshape(batch, n_head, 1, dk), jnp.broadcast_to(m0[:, :, None, None], (batch, n_head, 1, 128)),
      p['mlstm_norm_g'].reshape(1, -1), p['mlstm_skip'].reshape(1, -1))
    y, c_new, n_new, m_new = outs
    return y, c_new, n_new.reshape(batch, n_head, dk), m_new[:, :, 0, 0]


def _segment_matrix(n_seg, width, order):
    rows = np.zeros((n_seg, n_seg * width), np.float32)
    for i in range(n_seg):
        rows[order[i], i * width:(i + 1) * width] = 1.0
    return jnp.asarray(rows, BF16)


def _segment_scores(seg, kq):
    h1, h2, _ = _split3(kq)
    return _dot_nt(seg, h1) + _dot_nt(seg, h2)


def _key_head_matrices(n_key, n_head, rows):
    col = np.arange(n_key * n_head)
    spread = (col[None, :] // n_head == np.arange(n_key)[:, None]).astype(np.float32)
    own = (col[None, :] % n_head == (np.arange(rows) % n_head)[:, None]).astype(np.float32)
    return jnp.asarray(spread, BF16), jnp.asarray(spread.T, BF16), jnp.asarray(own)


def _spread_dot(a, own, spread, v2):
    return _dot((_dot(a.astype(BF16), spread) * own).astype(BF16), v2)


def _diff_sample_kernel(n_q, n_step, lam_init, pt_ref, q_ref, ka_ref, va_ref, kb_ref, vb_ref, kn_ref, vn_ref,
                        bias_ref, bias_new_ref, seg_ref, own_ref, spread_ref, lam_ref, g_ref, o_ref, m_s, l_s, a_s):
    p = pl.program_id(1)
    n_head = seg_ref.shape[0] // 2
    scale = D_HA ** -0.5

    @pl.when(p == 0)
    def _():
        m_s[...] = jnp.full(m_s.shape, NEG_BIG, F32)
        l_s[...] = jnp.zeros(l_s.shape, F32)
        a_s[...] = jnp.zeros(a_s.shape, F32)

    def scores(k_ref, bias):
        k = k_ref[...]
        sq = [_segment_scores(seg_ref[...], k * q_ref[i:i + 1, :]) for i in range(n_q)]
        return [jnp.concatenate([x[mi * n_head:(mi + 1) * n_head] for x in sq], 0) * scale + bias for mi in range(2)]

    def absorb(s_maps, v_ref):
        v2 = v_ref[...].reshape(-1, v_ref.shape[-1]).astype(BF16)
        for mi, s in enumerate(s_maps):
            m = m_s[mi]
            mn = jnp.maximum(m, jnp.max(s, -1, keepdims=True))
            pr = jnp.exp(s - mn)
            alpha = jnp.exp(m - mn)
            m_s[mi] = mn
            l_s[mi] = alpha * l_s[mi] + jnp.sum(pr, -1, keepdims=True)
            a_s[mi] = alpha * a_s[mi] + _spread_dot(pr, own_ref[...], spread_ref[...], v2)

    @pl.when(p < n_step)
    def _():
        s_a, s_b = scores(ka_ref, bias_ref[0]), scores(kb_ref, bias_ref[1])
        absorb(s_a, va_ref)
        absorb(s_b, vb_ref)

    @pl.when(p == n_step)
    def _():
        absorb(scores(kn_ref, bias_new_ref[...]), vn_ref)
        o_ref[...] = _diff_finish(a_s[0], l_s[0], a_s[1], l_s[1], _lambda(lam_ref, lam_init), g_ref[...], lam_init)


def _diff_attention_sample(q, k_new, v_new, cache_k, cache_v, page_table, rel_bias, lam_vec, g, lam_init):
    db, n_q, width = q.shape
    n_head = width // (2 * D_HA)
    dv = cache_v.shape[-1]
    n_page = page_table.shape[1]
    past = n_page * PAGE
    pad_new = lambda a: jnp.pad(a, ((0, 0), (0, PAGE - n_q), (0, 0)))
    qpos = past + jnp.arange(n_q, dtype=jnp.int32)
    kpos = jnp.arange(past + PAGE, dtype=jnp.int32)
    bias = rel_bias[_t5_bucket(qpos[:, None] - kpos[None, :])]
    bias = jnp.where((kpos[None, :] <= qpos[:, None])[..., None], bias, NEG_BIG)
    bias = jnp.transpose(bias, (0, 2, 1)).reshape(n_q * n_head, n_page + 1, PAGE)
    bias = jnp.transpose(bias, (1, 0, 2))
    seg = _segment_matrix(2 * n_head, D_HA, [(i % 2) * n_head + i // 2 for i in range(2 * n_head)])
    rows = n_q * n_head
    spread, _, own = _key_head_matrices(PAGE, n_head, rows)
    assert n_page % 2 == 0
    n_step = n_page // 2
    page_of = lambda j: (lambda b, p, pt: (pt[b, jnp.minimum(2 * p, n_page - 2) + j], 0, 0))
    page4_of = lambda j: (lambda b, p, pt: (pt[b, jnp.minimum(2 * p, n_page - 2) + j], 0, 0, 0))
    const = lambda b, p, pt: (0, 0)
    grid_spec = pltpu.PrefetchScalarGridSpec(
        num_scalar_prefetch=1,
        grid=(db, n_step + 1),
        in_specs=[pl.BlockSpec((None, n_q, width), lambda b, p, pt: (b, 0, 0)),
                  pl.BlockSpec((None, PAGE, width), page_of(0)),
                  pl.BlockSpec((None, PAGE, n_head, dv), page4_of(0)),
                  pl.BlockSpec((None, PAGE, width), page_of(1)),
                  pl.BlockSpec((None, PAGE, n_head, dv), page4_of(1)),
                  pl.BlockSpec((None, PAGE, width), lambda b, p, pt: (b, 0, 0)),
                  pl.BlockSpec((None, PAGE, n_head, dv), lambda b, p, pt: (b, 0, 0, 0)),
                  pl.BlockSpec((None, 2, rows, PAGE), lambda b, p, pt: (jnp.minimum(p, n_step - 1), 0, 0, 0)),
                  pl.BlockSpec((rows, PAGE), const),
                  pl.BlockSpec(seg.shape, const),
                  pl.BlockSpec(own.shape, const),
                  pl.BlockSpec(spread.shape, const),
                  pl.BlockSpec((4, D_HA), const),
                  pl.BlockSpec((1, dv), const)],
        out_specs=pl.BlockSpec((None, rows, dv), lambda b, p, pt: (b, 0, 0)),
        scratch_shapes=[pltpu.VMEM((2, rows, 1), F32), pltpu.VMEM((2, rows, 1), F32),
                        pltpu.VMEM((2, rows, dv), F32)])
    out = pl.pallas_call(
        functools.partial(_diff_sample_kernel, n_q, n_step, lam_init),
        grid_spec=grid_spec,
        out_shape=jax.ShapeDtypeStruct((db, rows, dv), F32),
        compiler_params=_cparams("parallel", "arbitrary"),
        name="diff_attn_sample",
    )(page_table, q, cache_k, cache_v, cache_k, cache_v, pad_new(k_new),
      pad_new(v_new).reshape(db, PAGE, n_head, dv), bias[:n_page].reshape(n_step, 2, rows, PAGE), bias[n_page],
      seg, own, spread, lam_vec, g.reshape(1, -1))
    return out.reshape(db, n_q, width)


def _sb_sample_kernel(scale, pt_ref, q_ref, ka_ref, va_ref, kb_ref, vb_ref, kn_ref, vn_ref, mask_ref, own_ref,
                      spread_ref, collapse_ref, o_ref, run_s, a_s):
    p = pl.program_id(1)
    keys, _, dh = ka_ref.shape
    suffix = _suffix_matrix(keys)
    q = q_ref[...].astype(BF16)

    def front(k_ref, mask):
        k2 = k_ref[...].reshape(-1, dh).astype(BF16)
        z = _dot_exact_rhs(_dot_nt(q, k2) * own_ref[...], collapse_ref[...], parts=2) * scale
        ls, lk = _log_sigmoid_pair(z)
        if mask is not None:
            lk = lk * mask
        return ls, _dot_exact_rhs(lk, suffix, parts=2), jnp.sum(lk, -1, keepdims=True)

    def absorb(fronts, v_refs, mask):
        run = run_s[...]
        total = a_s[...]
        for (ls, after, mass), v_ref in zip(fronts, v_refs):
            a = jnp.exp(ls + after + run)
            if mask is not None:
                a = a * mask
            total = total + _spread_dot(a, own_ref[...], spread_ref[...], v_ref[...].reshape(-1, dh).astype(BF16))
            run = run + mass
        run_s[...] = run
        a_s[...] = total

    @pl.when(p == 0)
    def _():
        run_s[...] = jnp.zeros(run_s.shape, F32)
        a_s[...] = jnp.zeros(a_s.shape, F32)
        absorb([front(kn_ref, mask_ref[...])], [vn_ref], mask_ref[...])

    @pl.when(p > 0)
    def _():
        absorb([front(ka_ref, None), front(kb_ref, None)], [va_ref, vb_ref], None)

    @pl.when(p == pl.num_programs(1) - 1)
    def _():
        o_ref[...] = a_s[...]


def _sb_attention_sample(q, k_new, v_new, cache_k, cache_v, page_table):
    db, n_q, width = q.shape
    n_head, dh = cache_k.shape[2:]
    n_page = page_table.shape[1]
    pad_new = lambda a: jnp.pad(a, ((0, 0), (0, PAGE - n_q), (0, 0))).reshape(db, PAGE, n_head, dh)
    qi = jnp.repeat(jnp.arange(n_q, dtype=jnp.int32), n_head)[:, None]
    mask = (jnp.arange(PAGE, dtype=jnp.int32)[None, :] < qi).astype(F32)
    rows = n_q * n_head
    spread, collapse, own = _key_head_matrices(PAGE, n_head, rows)
    assert n_page % 2 == 0
    page_of = lambda j: (lambda b, p, pt: (pt[b, n_page - 2 * jnp.maximum(p, 1) + j], 0, 0, 0))
    const = lambda b, p, pt: (0, 0)
    grid_spec = pltpu.PrefetchScalarGridSpec(
        num_scalar_prefetch=1,
        grid=(db, n_page // 2 + 1),
        in_specs=[pl.BlockSpec((None, rows, dh), lambda b, p, pt: (b, 0, 0)),
                  pl.BlockSpec((None, PAGE, n_head, dh), page_of(1)),
                  pl.BlockSpec((None, PAGE, n_head, dh), page_of(1)),
                  pl.BlockSpec((None, PAGE, n_head, dh), page_of(0)),
                  pl.BlockSpec((None, PAGE, n_head, dh), page_of(0)),
                  pl.BlockSpec((None, PAGE, n_head, dh), lambda b, p, pt: (b, 0, 0, 0)),
                  pl.BlockSpec((None, PAGE, n_head, dh), lambda b, p, pt: (b, 0, 0, 0)),
                  pl.BlockSpec(mask.shape, const),
                  pl.BlockSpec(own.shape, const),
                  pl.BlockSpec(spread.shape, const),
                  pl.BlockSpec(collapse.shape, const)],
        out_specs=pl.BlockSpec((None, rows, dh), lambda b, p, pt: (b, 0, 0)),
        scratch_shapes=[pltpu.VMEM((rows, 1), F32), pltpu.VMEM((rows, dh), F32)])
    out = pl.pallas_call(
        functools.partial(_sb_sample_kernel, dh ** -0.5),
        grid_spec=grid_spec,
        out_shape=jax.ShapeDtypeStruct((db, rows, dh), F32),
        compiler_params=_cparams("parallel", "arbitrary"),
        name="sb_attn_sample",
    )(page_table, q.reshape(db, rows, dh), cache_k, cache_v, cache_k, cache_v, pad_new(k_new), pad_new(v_new), mask,
      own, spread, collapse)
    return out.reshape(db, n_q, width)


def _top_k_rows(arrays, k):
    n, lanes = arrays[0].shape
    row = lax.broadcasted_iota(jnp.int32, (n, lanes), 0).astype(F32)
    slot = lax.broadcasted_iota(jnp.int32, (k, lanes), 0)

    def body(i, carry):
        out = []
        for s, vals, ids in carry:
            m = jnp.max(s, 0, keepdims=True)
            first = jnp.min(jnp.where(s == m, row, float(n)), 0, keepdims=True)
            out.append((jnp.where(row == first, -jnp.inf, s), jnp.where(slot == i, m, vals),
                        jnp.where(slot == i, first, ids)))
        return tuple(out)

    zero = jnp.zeros((k, lanes), F32)
    done = lax.fori_loop(0, k, body, tuple((s, zero, zero) for s in arrays))
    return [(vals, ids) for _, vals, ids in done]


def _staircase_rows(a, b, fn, fill):
    k = a.shape[0]
    rows = [fn(a[i:i + 1, :], b[0:k // (i + 1), :]) for i in range(k)]
    n = sum(r.shape[0] for r in rows)
    rows.append(jnp.full((-n % 8, a.shape[1]), fill, F32))
    return jnp.concatenate(rows, 0)


def _peer_route_kernel(wqt_ref, keys_ref, x_ref, idx_ref, g_ref):
    nk = keys_ref.shape[2]
    qt = _dot_nt(wqt_ref[...], x_ref[...].astype(BF16)).astype(BF16)
    half = keys_ref.shape[3]
    for h in range(keys_ref.shape[0]):
        scores = [_dot(keys_ref[h, p], qt[(h * 2 + p) * half:(h * 2 + p + 1) * half, :]) for p in range(2)]
        (s1, i1), (s2, i2) = _top_k_rows(scores, PEER_TOPK)
        cand = _staircase_rows(s1, s2, lambda a, b: a + b, -jnp.inf)
        expert = _staircase_rows(i1, i2, lambda a, b: a * nk + b, 0.0)
        (sc, ci), = _top_k_rows([cand], PEER_TOPK)
        crow = lax.broadcasted_iota(jnp.int32, cand.shape, 0).astype(F32)
        picked = [jnp.sum(jnp.where(crow == ci[j:j + 1, :], expert, 0.0), 0, keepdims=True) for j in range(PEER_TOPK)]
        e = jnp.exp(sc - jnp.max(sc, 0, keepdims=True))
        idx_ref[h] = jnp.concatenate(picked, 0).astype(jnp.int32)
        g_ref[h] = e / jnp.sum(e, 0, keepdims=True)


def _peer_route(x, wqt_bf16, subkeys_bf16):
    t, d = x.shape
    heads = subkeys_bf16.shape[0]
    tt = 128
    return pl.pallas_call(
        _peer_route_kernel,
        grid=(t // tt,),
        in_specs=[pl.BlockSpec(wqt_bf16.shape, lambda i: (0, 0)),
                  pl.BlockSpec(subkeys_bf16.shape, lambda i: (0, 0, 0, 0)),
                  pl.BlockSpec((tt, d), lambda i: (i, 0))],
        out_specs=[pl.BlockSpec((heads, PEER_TOPK, tt), lambda i: (0, 0, i)),
                   pl.BlockSpec((heads, PEER_TOPK, tt), lambda i: (0, 0, i))],
        out_shape=[jax.ShapeDtypeStruct((heads, PEER_TOPK, t), jnp.int32),
                   jax.ShapeDtypeStruct((heads, PEER_TOPK, t), F32)],
        compiler_params=_cparams("parallel"),
        name="peer_route",
    )(wqt_bf16, subkeys_bf16, x)


PEER_TOKEN_CHUNK = 1024
PEER_EXPERT_TILE = 256
PEER_PAIR_BLOCK = 1024


def _gelu(x):
    return 0.5 * x * (1.0 + lax.erf(x * (2.0 ** -0.5)))


def _peer_expert_kernel(rpt, meta_ref, trow_ref, erow_ref, gate_ref, x_ref, u_ref, v_ref, o_ref, stage, act):
    i = pl.program_id(0)
    first, lo, hi = meta_ref[2, i], meta_ref[3, i], meta_ref[4, i]

    @pl.when(first == 1)
    def _():
        o_ref[...] = jnp.zeros(o_ref.shape, F32)

    @pl.when(i == 0)
    def _():
        stage[...] = jnp.zeros(stage.shape, F32)

    @pl.when(hi > lo)
    def _():
        _peer_segment(rpt, trow_ref, erow_ref, gate_ref, x_ref, u_ref, v_ref, o_ref, stage, act, lo, hi)


def _peer_segment(rpt, trow_ref, erow_ref, gate_ref, x_ref, u_ref, v_ref, o_ref, stage, act, lo, hi):
    def rows(ref, j):
        return pl.ds(pl.multiple_of(ref[0, j], rpt), rpt)

    def sweep(one, group, width):
        g_lo, g_hi = (lo + width - 1) // width, hi // width
        head_end = jnp.minimum(g_lo * width, hi)
        tail_start = jnp.maximum(g_hi * width, head_end)
        lax.fori_loop(lo, head_end, lambda j, c: (one(j), c)[1], 0)
        lax.fori_loop(g_lo, g_hi, lambda s, c: (group(s), c)[1], 0)
        lax.fori_loop(tail_start, hi, lambda j, c: (one(j), c)[1], 0)

    def dot_pair(j, row):
        pr = x_ref[rows(trow_ref, j), :] * u_ref[rows(erow_ref, j), :]
        acc = pr[0:8]
        for c in range(1, rpt // 8):
            acc = acc + pr[c * 8:(c + 1) * 8]
        stage[pl.ds(row, 8), :] = acc

    def dot_group(s):
        base = pl.multiple_of(s * 128, 128)
        for r in range(16):
            dot_pair(s * 16 + r, base + r * 8)

    sweep(lambda j: dot_pair(j, pl.multiple_of(j * 8, 8)), dot_group, 16)

    rr = lax.broadcasted_iota(jnp.int32, (128, 128), 0)
    cc = lax.broadcasted_iota(jnp.int32, (128, 128), 1)
    ones = jnp.ones((128, 128), BF16)

    def weights(g):
        base = pl.multiple_of(g * 1024, 1024)
        part = stage[pl.ds(base, 128, stride=8), :]
        for s in range(1, 8):
            part = part + stage[pl.ds(base + s, 128, stride=8), :]
        a = _gelu(_dot_exact_rhs(part, ones, parts=2))
        gate = _dot_exact_rhs(jnp.where(rr == cc, gate_ref[pl.ds(g, 1), :], 0.0), ones, parts=2)
        act[pl.ds(pl.multiple_of(g * 16, 16), 16)] = (a * gate).reshape(16, 8, 128)

    lax.fori_loop(lo // 256, (hi + 255) // 256, lambda t, c: (weights(2 * t), weights(2 * t + 1), c)[2], 0)

    def mix_one(j):
        out = rows(trow_ref, j)
        o_ref[out, :] = o_ref[out, :] + v_ref[rows(erow_ref, j), :] * act[j // 8, pl.ds(j % 8, 1), :]

    def mix_group(s):
        toks, news = [], []
        acc = None
        for r in range(8):
            j = s * 8 + r
            tok = trow_ref[0, j]
            c = v_ref[rows(erow_ref, j), :] * act[s, r:r + 1, :]
            acc = c if r == 0 else jnp.where(tok == toks[-1], acc, 0.0) + c
            toks.append(tok)
            news.append(o_ref[pl.ds(pl.multiple_of(tok, rpt), rpt), :] + acc)
        for tok, new in zip(toks, news):
            o_ref[pl.ds(pl.multiple_of(tok, rpt), rpt), :] = new

    sweep(mix_one, mix_group, 8)


def _peer_plan(idx, gates, t_pad, rpt):
    heads, topk, t = idx.shape
    pb = PEER_PAIR_BLOCK
    e_bits = PEER_EXPERT_TILE.bit_length() - 1
    c_bits = e_bits + PEER_TOKEN_CHUNK.bit_length() - 1
    n_tile = (PEER_NK * PEER_NK) // PEER_EXPERT_TILE
    n_cell = (t_pad // PEER_TOKEN_CHUNK) * n_tile
    n_chunk = t_pad // PEER_TOKEN_CHUNK
    per_chunk = PEER_TOKEN_CHUNK * heads * topk
    assert per_chunk % pb == 0 and ((n_cell + 1) << c_bits) < 2 ** 31
    n_blk = n_chunk * per_chunk // pb
    e = jnp.transpose(idx, (2, 0, 1)).reshape(-1)
    g = jnp.transpose(gates, (2, 0, 1)).reshape(-1)
    tok = jnp.repeat(jnp.arange(t, dtype=jnp.int32), heads * topk)
    cell = (tok // PEER_TOKEN_CHUNK) * n_tile + e // PEER_EXPERT_TILE
    key = (cell << c_bits) | ((tok % PEER_TOKEN_CHUNK) << e_bits) | (e % PEER_EXPERT_TILE)
    fill = n_chunk * per_chunk - key.shape[0]
    key = jnp.pad(key, (0, fill), constant_values=n_cell << c_bits).reshape(n_chunk, per_chunk)
    key_s, g_s = lax.sort((key, jnp.pad(g, (0, fill)).reshape(n_chunk, per_chunk)), dimension=1, num_keys=1,
                          is_stable=False)
    key_s, g_s = key_s.reshape(-1), g_s.reshape(-1)
    bounds = jnp.searchsorted(key_s, jnp.arange(n_cell + 1, dtype=jnp.int32) << c_bits, side='left').astype(jnp.int32)
    first_blk = bounds[:-1] // pb
    n_seg_cell = jnp.where(bounds[1:] > bounds[:-1], (bounds[1:] - 1) // pb - first_blk + 1, 0)
    seg_end = jnp.cumsum(n_seg_cell)
    step = jnp.arange(n_blk + n_cell, dtype=jnp.int32)
    last = seg_end[-1] - 1
    step_c = jnp.minimum(step, last)
    q = jnp.minimum(jnp.searchsorted(seg_end, step_c, side='right'), n_cell - 1).astype(jnp.int32)
    blk = first_blk[q] + step_c - (seg_end - n_seg_cell)[q]
    lo = jnp.where(step <= last, jnp.clip(bounds[q] - blk * pb, 0, pb), 0)
    hi = jnp.where(step <= last, jnp.clip(bounds[q + 1] - blk * pb, 0, pb), 0)
    chunk = q // n_tile
    first = jnp.concatenate([jnp.ones((1,), jnp.int32), (chunk[1:] != chunk[:-1]).astype(jnp.int32)])
    meta = jnp.stack([chunk, q % n_tile, first, lo, hi, blk]).astype(jnp.int32)
    trow = ((key_s >> e_bits) & (PEER_TOKEN_CHUNK - 1)) * rpt
    erow = (key_s & (PEER_EXPERT_TILE - 1)) * rpt
    return meta, trow.reshape(n_blk, 1, pb), erow.reshape(n_blk, 1, pb), g_s.reshape(n_blk, pb // 128, 128)


def _peer_experts(x, idx, gates, u, v):
    t, d = x.shape
    t_pad = -(-t // PEER_TOKEN_CHUNK) * PEER_TOKEN_CHUNK
    rpt = d // 128
    meta, trow, erow, gate = _peer_plan(idx, gates, t_pad, rpt)
    rows = lambda a: a.reshape(a.shape[0] * rpt, 128)
    pairs = pl.BlockSpec((None, 1, PEER_PAIR_BLOCK), lambda i, m: (m[5, i], 0, 0), memory_space=pltpu.SMEM)
    tokens = pl.BlockSpec((PEER_TOKEN_CHUNK * rpt, 128), lambda i, m: (m[0, i], 0))
    experts = pl.BlockSpec((PEER_EXPERT_TILE * rpt, 128), lambda i, m: (m[1, i], 0))
    grid_spec = pltpu.PrefetchScalarGridSpec(
        num_scalar_prefetch=1,
        grid=(meta.shape[1],),
        in_specs=[pairs, pairs,
                  pl.BlockSpec((None, PEER_PAIR_BLOCK // 128, 128), lambda i, m: (m[5, i], 0, 0)),
                  tokens, experts, experts],
        out_specs=tokens,
        scratch_shapes=[pltpu.VMEM((PEER_PAIR_BLOCK * 8, 128), F32),
                        pltpu.VMEM((PEER_PAIR_BLOCK // 8, 8, 128), F32)])
    out = pl.pallas_call(
        functools.partial(_peer_expert_kernel, rpt),
        grid_spec=grid_spec,
        out_shape=jax.ShapeDtypeStruct((t_pad * rpt, 128), F32),
        compiler_params=pltpu.CompilerParams(dimension_semantics=("arbitrary",),
                                             vmem_limit_bytes=56 * 1024 * 1024),
        name="peer_experts",
    )(meta, trow, erow, gate, rows(jnp.pad(x, ((0, t_pad - t), (0, 0)))), rows(u), rows(v))
    return out.reshape(t_pad, d)[:t]


def _peer_layer(x, wq, subkeys, u, v, g, b, alpha):
    idx, gates = _peer_route(x, wq.T.astype(BF16), subkeys.astype(BF16))
    return _res_ln(x, _peer_experts(x, idx, gates, u, v), g, b, alpha)


def _pad_cols(w, mult):
    return jnp.pad(w, ((0, 0), (0, -w.shape[1] % mult)))


def kernel(x_prompt, x_sample, cache_diff_k, cache_diff_v, state_mlstm_C, state_mlstm_n, state_mlstm_m, state_mlstm_conv, cache_sb_k, cache_sb_v, page_table, rel_bias, w_in_even, b_ig, b_fg, lam_q1, lam_k1, lam_q2, lam_k2, diff_norm_g, conv_w, conv_b, w_q_mlstm, w_k_mlstm, mlstm_norm_g, mlstm_skip, w_out_even, w_qkv_odd, w_out_odd, ln_g, ln_b, peer_wq, peer_subkeys, peer_u, peer_v):
    nb, seq, d = x_prompt.shape
    db, n_q, _ = x_sample.shape
    tp, ts = nb * seq, db * n_q
    n_pool = cache_diff_k.shape[1]
    depth = ln_g.shape[0]
    alpha = (2.0 * depth) ** 0.25
    h_a, h_b, h_c = d // 256, w_q_mlstm.shape[1], d // 128
    w_a, w_b = h_a * 2 * D_HA, d // 2
    assert depth == 2 and w_in_even.shape[0] == 1 and w_qkv_odd.shape[0] == 1 and w_a == w_b
    lam_init = 0.8 - 0.6 * math.exp(-0.3 * 0)
    x = jnp.concatenate([x_prompt.reshape(tp, d), x_sample.reshape(ts, d)])

    n_main = 3 * w_a + 3 * w_b
    proj = _matmul(x, _pad_cols(w_in_even[0], 896).astype(BF16), 896)
    qa, ka, va, u_b = (proj[:, i * w_a:(i + 1) * w_a] for i in range(4))
    gates = proj[:, n_main:n_main + 2 * h_b]
    lam_vec = jnp.stack([lam_q1[0], lam_k1[0], lam_q2[0], lam_k2[0]])
    ya_p = _diff_attention_prompt(proj, nb, seq, h_a, rel_bias, lam_vec, diff_norm_g[0], lam_init, 256)
    smp = lambda a: a[tp:].reshape(db, n_q, -1)
    ya_s = _diff_attention_sample(smp(qa), smp(ka), smp(va), cache_diff_k[0].reshape(n_pool, PAGE, w_a),
                                  cache_diff_v[0], page_table, rel_bias, lam_vec,
                                  diff_norm_g[0], lam_init)
    mp = {'b_ig': b_ig[0], 'b_fg': b_fg[0], 'conv_w': conv_w[0], 'conv_b': conv_b[0], 'w_q_mlstm': w_q_mlstm[0],
          'w_k_mlstm': w_k_mlstm[0], 'mlstm_norm_g': mlstm_norm_g[0], 'mlstm_skip': mlstm_skip[0]}
    lane0 = 3 * w_a // 256
    yb_p, c_p, n_p, m_p = _mlstm(proj, gates[:tp], nb, seq, MLSTM_CHUNK, h_b, lane0,
                                 jnp.zeros((nb, 8, w_b), F32), jnp.zeros((nb, h_b, 256, 256), F32),
                                 jnp.zeros((nb, h_b, 256), F32), jnp.zeros((nb, h_b), F32), mp)
    s_pad = 8
    pad_t = lambda a, val=0.0: jnp.pad(a, ((0, 0), (0, s_pad - n_q), (0, 0)), constant_values=val)
    gate_s = smp(gates)
    gate_s = jnp.concatenate([pad_t(gate_s[..., :h_b], NEG_BIG), pad_t(gate_s[..., h_b:], -NEG_BIG)], -1)
    conv0 = jnp.pad(state_mlstm_conv[0], ((0, 0), (8 - (CONV_W - 1), 0), (0, 0)))
    yb_s, c_s, n_s, m_s = _mlstm(pad_t(smp(proj)).reshape(db * s_pad, -1), gate_s.reshape(db * s_pad, -1), db, s_pad,
                                 s_pad, h_b, lane0, conv0, state_mlstm_C[0], state_mlstm_n[0], state_mlstm_m[0], mp)
    yb_s = yb_s.reshape(db, s_pad, w_b)[:, :n_q].reshape(ts, w_b)
    y = jnp.concatenate([jnp.concatenate([ya_p, yb_p], -1), jnp.concatenate([ya_s.reshape(ts, w_a), yb_s], -1)])
    x = _matmul_res_ln(y, w_out_even[0].astype(BF16), x, ln_g[0, 0], ln_b[0, 0], alpha)
    x = _peer_layer(x, peer_wq[0], peer_subkeys[0], peer_u[0], peer_v[0], ln_g[0, 1], ln_b[0, 1], alpha)

    qkv = _matmul(x, w_qkv_odd[0].astype(BF16), 1024)
    y_p = _sb_attention_prompt(qkv, nb, seq, h_c, 128, 256)
    q_s, k_s, v_s = (smp(qkv[:, i * d:(i + 1) * d]) for i in range(3))
    y_s = _sb_attention_sample(q_s, k_s, v_s, cache_sb_k[0], cache_sb_v[0], page_table)
    y = jnp.concatenate([y_p, y_s.reshape(ts, d)])
    x = _matmul_res_ln(y, w_out_odd[0].astype(BF16), x, ln_g[1, 0], ln_b[1, 0], alpha)
    x = _peer_layer(x, peer_wq[1], peer_subkeys[1], peer_u[1], peer_v[1], ln_g[1, 1], ln_b[1, 1], alpha)

    prm = lambda a, *s: a[:tp].reshape((1, nb, seq) + s)
    u_p = proj[:tp, 3 * w_a:3 * w_a + w_b].reshape(nb, seq, w_b)
    u_s = smp(u_b)
    sbk, sbv = qkv[:, d:2 * d], qkv[:, 2 * d:]
    return (x[:tp].reshape(nb, seq, d), x[tp:].reshape(db, n_q, d),
            prm(ka, h_a, 2, D_HA), prm(va, h_a, 2 * D_HA),
            smp(ka).reshape(1, db, n_q, h_a, 2, D_HA), smp(va).reshape(1, db, n_q, h_a, 2 * D_HA),
            c_p[None], n_p[None], m_p[None], u_p[None, :, seq - (CONV_W - 1):],
            c_s[None], n_s[None], m_s[None], u_s[None, :, n_q - (CONV_W - 1):],
            prm(sbk, h_c, 128), prm(sbv, h_c, 128),
            smp(sbk).reshape(1, db, n_q, h_c, 128), smp(sbv).reshape(1, db, n_q, h_c, 128))
```

```python
import functools
import math

import jax
import jax.numpy as jnp
import numpy as np
from jax import lax
from jax.experimental import pallas as pl
from jax.experimental.pallas import tpu as pltpu

F32 = jnp.float32
BF16 = jnp.bfloat16

LN_EPS = 1e-5
NEG_BIG = -1e30
VMEM_LIMIT = 48 * 1024 * 1024

D_HA = 64
N_BUCKETS = 32
MAX_DIST = 128
CONV_W = 4
MLSTM_CHUNK = 128
PAGE = 128
PEER_HEADS = 8
PEER_NK = 128
PEER_TOPK = 16


def _cparams(*sem):
    return pltpu.CompilerParams(dimension_semantics=sem, vmem_limit_bytes=VMEM_LIMIT)


def _dot(a, b):
    return jnp.dot(a, b, preferred_element_type=F32)


def _dot_nt(a, b):
    return lax.dot_general(a, b, (((1,), (1,)), ((), ())), preferred_element_type=F32)


def _dot_tn(a, b):
    return lax.dot_general(a, b, (((0,), (0,)), ((), ())), preferred_element_type=F32)


def _split3(x):
    h1 = x.astype(BF16)
    r1 = x - h1.astype(F32)
    h2 = r1.astype(BF16)
    h3 = (r1 - h2.astype(F32)).astype(BF16)
    return h1, h2, h3


def _dot_exact_rhs(x, m_bf16, parts=3):
    return sum(_dot(h, m_bf16) for h in _split3(x)[:parts])


def _row_tile(m):
    for t in (384, 256, 128, 64, 32, 16, 8):
        if m % t == 0:
            return t
    raise ValueError(f"row count {m} is not a multiple of 8")


def _mm_kernel(x_ref, w_ref, o_ref):
    o_ref[...] = _dot(x_ref[...].astype(BF16), w_ref[...])


def _matmul(x, w_bf16, tn):
    m, k = x.shape
    n = w_bf16.shape[1]
    tm = _row_tile(m)
    return pl.pallas_call(
        _mm_kernel,
        grid=(m // tm, n // tn),
        in_specs=[pl.BlockSpec((tm, k), lambda i, j: (i, 0)),
                  pl.BlockSpec((k, tn), lambda i, j: (0, j))],
        out_specs=pl.BlockSpec((tm, tn), lambda i, j: (i, j)),
        out_shape=jax.ShapeDtypeStruct((m, n), F32),
        compiler_params=_cparams("parallel", "arbitrary"),
        name="matmul",
    )(x, w_bf16)


def _mm_nt_kernel(w_ref, x_ref, o_ref):
    o_ref[...] = _dot_nt(w_ref[...], x_ref[...].astype(BF16))


def _matmul_t(wt_bf16, x):
    n, k = wt_bf16.shape
    m = x.shape[0]
    tm = _row_tile(m)
    return pl.pallas_call(
        _mm_nt_kernel,
        grid=(m // tm,),
        in_specs=[pl.BlockSpec((n, k), lambda i: (0, 0)),
                  pl.BlockSpec((tm, k), lambda i: (i, 0))],
        out_specs=pl.BlockSpec((n, tm), lambda i: (0, i)),
        out_shape=jax.ShapeDtypeStruct((n, m), F32),
        compiler_params=_cparams("parallel"),
        name="matmul_t",
    )(wt_bf16, x)


def _layer_norm(xf, g, b):
    mu = jnp.mean(xf, -1, keepdims=True)
    xc = xf - mu
    var = jnp.mean(xc * xc, -1, keepdims=True)
    return xc * lax.rsqrt(var + LN_EPS) * g + b


def _mm_res_ln_kernel(alpha, a_ref, w_ref, x_ref, g_ref, b_ref, o_ref):
    y = _dot(a_ref[...].astype(BF16), w_ref[...])
    o_ref[...] = _layer_norm(alpha * x_ref[...] + y, g_ref[...], b_ref[...])


def _matmul_res_ln(a, w_bf16, x, g, b, alpha):
    m, k = a.shape
    n = w_bf16.shape[1]
    tm = _row_tile(m)
    return pl.pallas_call(
        functools.partial(_mm_res_ln_kernel, alpha),
        grid=(m // tm,),
        in_specs=[pl.BlockSpec((tm, k), lambda i: (i, 0)),
                  pl.BlockSpec((k, n), lambda i: (0, 0)),
                  pl.BlockSpec((tm, n), lambda i: (i, 0)),
                  pl.BlockSpec((1, n), lambda i: (0, 0)),
                  pl.BlockSpec((1, n), lambda i: (0, 0))],
        out_specs=pl.BlockSpec((tm, n), lambda i: (i, 0)),
        out_shape=jax.ShapeDtypeStruct((m, n), F32),
        compiler_params=_cparams("parallel"),
        name="matmul_res_ln",
    )(a, w_bf16, x, g.reshape(1, n), b.reshape(1, n))


def _res_ln_kernel(alpha, x_ref, f_ref, g_ref, b_ref, o_ref):
    o_ref[...] = _layer_norm(alpha * x_ref[...] + f_ref[...], g_ref[...], b_ref[...])


def _res_ln(x, f, g, b, alpha):
    m, n = x.shape
    tm = _row_tile(m)
    return pl.pallas_call(
        functools.partial(_res_ln_kernel, alpha),
        grid=(m // tm,),
        in_specs=[pl.BlockSpec((tm, n), lambda i: (i, 0)),
                  pl.BlockSpec((tm, n), lambda i: (i, 0)),
                  pl.BlockSpec((1, n), lambda i: (0, 0)),
                  pl.BlockSpec((1, n), lambda i: (0, 0))],
        out_specs=pl.BlockSpec((tm, n), lambda i: (i, 0)),
        out_shape=jax.ShapeDtypeStruct((m, n), F32),
        compiler_params=_cparams("parallel"),
        name="res_ln",
    )(x, f, g.reshape(1, n), b.reshape(1, n))


def _t5_bucket(dist):
    n = jnp.maximum(dist, 0)
    exact = N_BUCKETS // 2
    large = exact + (jnp.log(jnp.maximum(n, exact).astype(F32) / exact)
                     / math.log(MAX_DIST / exact) * (N_BUCKETS - exact)).astype(jnp.int32)
    return jnp.where(n < exact, n, jnp.minimum(large, N_BUCKETS - 1))


def _lambda(lam_ref, lam_init):
    lq1, lk1, lq2, lk2 = lam_ref[0:1, :], lam_ref[1:2, :], lam_ref[2:3, :], lam_ref[3:4, :]
    return (jnp.exp(jnp.sum(lq1 * lk1, keepdims=True)) - jnp.exp(jnp.sum(lq2 * lk2, keepdims=True))
            + lam_init)


def _diff_finish(acc1, l1, acc2, l2, lam, g, lam_init):
    o = acc1 / l1 - lam * (acc2 / l2)
    ms = jnp.mean(o * o, -1, keepdims=True)
    return o * lax.rsqrt(ms + LN_EPS) * g * (1.0 - lam_init)


def _diff_prompt_kernel(tq, lam_init, q_ref, k_ref, v_ref, bias_ref, far_ref, lam_ref, g_ref, o_ref):
    qi = pl.program_id(2)
    scale = D_HA ** -0.5
    q = q_ref[...]
    lane = lax.broadcasted_iota(jnp.int32, q.shape, 1)
    q1 = jnp.where(lane < D_HA, q, 0.0).astype(BF16)
    q2 = jnp.where(lane >= D_HA, q, 0.0).astype(BF16)
    dv = v_ref.shape[-1]

    def update(state, ki, bias, causal):
        m1, l1, a1, m2, l2, a2 = state
        start = pl.multiple_of(ki * tq, tq)
        kb = k_ref[pl.ds(start, tq), :].astype(BF16)
        vb = v_ref[pl.ds(start, tq), :].astype(BF16)
        out = []
        for qm, m, l, a in ((q1, m1, l1, a1), (q2, m2, l2, a2)):
            s = _dot_nt(qm, kb) * scale + bias
            if causal:
                r = lax.broadcasted_iota(jnp.int32, s.shape, 0)
                c = lax.broadcasted_iota(jnp.int32, s.shape, 1)
                s = jnp.where(c <= r, s, NEG_BIG)
            mn = jnp.maximum(m, jnp.max(s, -1, keepdims=True))
            p = jnp.exp(s - mn)
            alpha = jnp.exp(m - mn)
            out += [mn, alpha * l + jnp.sum(p, -1, keepdims=True), alpha * a + _dot(p.astype(BF16), vb)]
        return tuple(out)

    init = (jnp.full((tq, 1), NEG_BIG, F32), jnp.zeros((tq, 1), F32), jnp.zeros((tq, dv), F32)) * 2
    state = update(init, qi, bias_ref[0], True)
    sub = jnp.where(qi >= 1, bias_ref[1], NEG_BIG)
    state = update(state, jnp.maximum(qi - 1, 0), sub, False)
    far = far_ref[...]
    n_far = jnp.maximum(qi - 1, 0)
    state = lax.fori_loop(0, n_far // 2,
                          lambda j, st: update(update(st, 2 * j, far, False), 2 * j + 1, far, False), state)
    m1, l1, a1, m2, l2, a2 = lax.fori_loop(0, n_far % 2, lambda j, st: update(st, n_far - 1, far, False), state)
    o_ref[...] = _diff_finish(a1, l1, a2, l2, _lambda(lam_ref, lam_init), g_ref[...], lam_init)


def _diff_bias_tables(rel_bias, tq):
    h = rel_bias.shape[1]
    r = jnp.arange(tq, dtype=jnp.int32)[:, None]
    c = jnp.arange(tq, dtype=jnp.int32)[None, :]
    tiles = jnp.stack([rel_bias[_t5_bucket(r - c)], rel_bias[_t5_bucket(r - c + tq)]])
    return jnp.transpose(tiles, (3, 0, 1, 2)), rel_bias[N_BUCKETS - 1].reshape(h, 1, 1)


def _diff_attention_prompt(proj, batch, seq, n_head, rel_bias, lam_vec, g, lam_init, tq):
    assert tq >= MAX_DIST and seq % tq == 0
    dh = 2 * D_HA
    nq = seq // tq
    bias_tiles, far = _diff_bias_tables(rel_bias, tq)
    return pl.pallas_call(
        functools.partial(_diff_prompt_kernel, tq, lam_init),
        grid=(batch, n_head, nq),
        in_specs=[pl.BlockSpec((tq, dh), lambda b, h, i: (b * nq + i, h)),
                  pl.BlockSpec((seq, dh), lambda b, h, i: (b, n_head + h)),
                  pl.BlockSpec((seq, dh), lambda b, h, i: (b, 2 * n_head + h)),
                  pl.BlockSpec((None, 2, tq, tq), lambda b, h, i: (h, 0, 0, 0)),
                  pl.BlockSpec((None, 1, 1), lambda b, h, i: (h, 0, 0)),
                  pl.BlockSpec((4, D_HA), lambda b, h, i: (0, 0)),
                  pl.BlockSpec((1, dh), lambda b, h, i: (0, 0))],
        out_specs=pl.BlockSpec((tq, dh), lambda b, h, i: (b * nq + i, h)),
        out_shape=jax.ShapeDtypeStruct((batch * seq, n_head * dh), F32),
        compiler_params=_cparams("parallel", "parallel", "arbitrary"),
        name="diff_attn_prompt",
    )(proj, proj, proj, bias_tiles, far, lam_vec, g.reshape(1, dh))


def _log_sigmoid_pair(z):
    t = jnp.log(1.0 + jnp.exp(-jnp.abs(z)))
    return jnp.minimum(z, 0.0) - t, jnp.minimum(-z, 0.0) - t


def _suffix_matrix(n):
    j = lax.broadcasted_iota(jnp.int32, (n, n), 0)
    s = lax.broadcasted_iota(jnp.int32, (n, n), 1)
    return jnp.where(j > s, 1.0, 0.0).astype(BF16)


def _sb_prompt_kernel(tq, scale, q_ref, k_ref, v_ref, o_ref):
    qi = pl.program_id(2)
    q = q_ref[...].astype(BF16)
    dv = v_ref.shape[-1]
    suffix = _suffix_matrix(tq)

    def update(state, ki, causal):
        run, acc = state
        start = pl.multiple_of(ki * tq, tq)
        kb = k_ref[pl.ds(start, tq), :].astype(BF16)
        vb = v_ref[pl.ds(start, tq), :].astype(BF16)
        z = _dot_nt(q, kb) * scale
        ls, lk = _log_sigmoid_pair(z)
        if causal:
            r = lax.broadcasted_iota(jnp.int32, z.shape, 0)
            c = lax.broadcasted_iota(jnp.int32, z.shape, 1)
            mask = c < r
            lk = jnp.where(mask, lk, 0.0)
        after = _dot_exact_rhs(lk, suffix, parts=2) + run
        a = jnp.exp(ls + after)
        if causal:
            a = jnp.where(mask, a, 0.0)
        return run + jnp.sum(lk, -1, keepdims=True), acc + _dot(a.astype(BF16), vb)

    state = update((jnp.zeros((tq, 1), F32), jnp.zeros((tq, dv), F32)), qi, True)
    state = lax.fori_loop(0, qi // 2,
                          lambda j, st: update(update(st, qi - 1 - 2 * j, False), qi - 2 - 2 * j, False), state)
    _, acc = lax.fori_loop(0, qi % 2, lambda j, st: update(st, 0, False), state)
    o_ref[...] = acc


def _sb_attention_prompt(qkv, batch, seq, n_head, dh, tq):
    nq = seq // tq
    return pl.pallas_call(
        functools.partial(_sb_prompt_kernel, tq, dh ** -0.5),
        grid=(batch, n_head, nq),
        in_specs=[pl.BlockSpec((tq, dh), lambda b, h, i: (b * nq + i, h)),
                  pl.BlockSpec((seq, dh), lambda b, h, i: (b, n_head + h)),
                  pl.BlockSpec((seq, dh), lambda b, h, i: (b, 2 * n_head + h))],
        out_specs=pl.BlockSpec((tq, dh), lambda b, h, i: (b * nq + i, h)),
        out_shape=jax.ShapeDtypeStruct((batch * seq, n_head * dh), F32),
        compiler_params=_cparams("parallel", "parallel", "arbitrary"),
        name="sb_attn_prompt",
    )(qkv, qkv, qkv)


def _mlstm_kernel(L, u_ref, v_ref, ob_ref, grow_ref, bcol_ref, conv0_ref, cw_ref, cb_ref,
                  wq_ref, wk_ref, c0_ref, n0_ref, m0_ref, g_ref, skip_ref,
                  y_ref, c_out, n_out, m_out, ext, c_s, n_s, m_s):
    c = pl.program_id(2)
    dk = u_ref.shape[-1]

    @pl.when(c == 0)
    def _():
        ext[0:8, :] = conv0_ref[...]
        c_s[...] = c0_ref[...]
        n_s[...] = n0_ref[...]
        m_s[...] = m0_ref[...]

    ext[8:8 + L, :] = u_ref[...]
    uc = cb_ref[...]
    for j in range(CONV_W):
        uc = uc + ext[pl.ds(8 - (CONV_W - 1) + j, L), :] * cw_ref[j:j + 1, :]
    ext[0:8, :] = ext[L:L + 8, :]
    ua = uc * jax.nn.sigmoid(uc)
    uab = ua.astype(BF16)
    qf = _dot(uab, wq_ref[...])
    q = qf.astype(BF16)
    kf = _dot(uab, wk_ref[...]) * dk ** -0.5
    k = kf.astype(BF16)
    vb = v_ref[...].astype(BF16)

    gr = grow_ref[...] + bcol_ref[...]
    li_r, lf_r = gr[0:1, :], _log_sigmoid_pair(gr[1:2, :])[0]
    t = lax.broadcasted_iota(jnp.int32, (L, L), 0)
    s = lax.broadcasted_iota(jnp.int32, (L, L), 1)
    causal = s <= t
    li_c = jnp.sum(jnp.where(s == t, li_r, 0.0), -1, keepdims=True)
    b_c = _dot_exact_rhs(jnp.where(causal, lf_r, 0.0), jnp.ones((L, 8), BF16))[:, 0:1]
    b_r = _dot_exact_rhs(jnp.broadcast_to(lf_r, (8, L)), jnp.where(t <= s, 1.0, 0.0).astype(BF16))[0:1, :]

    m = m_s[:, 0:1]
    cmat = c_s[...]
    nrow = n_s[...]
    d = jnp.where(causal, b_c - b_r + li_r, NEG_BIG)
    inter = b_c + m
    mt = jnp.maximum(inter, jnp.max(d, -1, keepdims=True))
    w = jnp.exp(d - mt) * _dot_nt(q, k)
    sc = jnp.exp(inter - mt)
    num = _dot(w.astype(BF16), vb) + sc * _dot(q, cmat.astype(BF16))
    den = jnp.sum(w, -1, keepdims=True) + sc * jnp.sum(qf * nrow, -1, keepdims=True)
    h = num / jnp.maximum(jnp.abs(den), jnp.exp(-mt))

    m_new = mt[L - 1:L, :]
    b_last = b_c[L - 1:L, :]
    wl = jnp.exp(b_last - b_c + li_c - m_new)
    dec = jnp.exp(b_last + m - m_new)
    kw = kf * wl
    c_s[...] = dec * cmat + _dot_tn(kw.astype(BF16), vb)
    n_s[...] = dec * nrow + jnp.sum(kw, 0, keepdims=True)
    m_s[...] = jnp.broadcast_to(m_new, m_s.shape)

    mu = jnp.mean(h, -1, keepdims=True)
    hc = h - mu
    var = jnp.mean(hc * hc, -1, keepdims=True)
    hn = hc * lax.rsqrt(var + LN_EPS) * g_ref[...]
    y_ref[...] = jax.nn.sigmoid(ob_ref[...]) * (hn + skip_ref[...] * ua)

    @pl.when(c == pl.num_programs(2) - 1)
    def _():
        c_out[...] = c_s[...]
        n_out[...] = n_s[...]
        m_out[...] = m_s[...]


def _mlstm(rows, gates, batch, seq, L, n_head, lane0, conv0, c0, n0, m0, p):
    dk = 256
    nc = seq // L
    t_rows = batch * seq
    grow = jnp.transpose(gates.reshape(batch * nc, L, 2, n_head), (3, 0, 2, 1))
    bias = jnp.stack([p['b_ig'], p['b_fg']], -1)
    outs = pl.pallas_call(
        functools.partial(_mlstm_kernel, L),
        grid=(batch, n_head, nc),
        in_specs=[pl.BlockSpec((L, dk), lambda b, h, c: (b * nc + c, lane0 + h)),
                  pl.BlockSpec((L, dk), lambda b, h, c: (b * nc + c, lane0 + n_head + h)),
                  pl.BlockSpec((L, dk), lambda b, h, c: (b * nc + c, lane0 + 2 * n_head + h)),
                  pl.BlockSpec((None, None, 2, L), lambda b, h, c: (h, b * nc + c, 0, 0)),
                  pl.BlockSpec((None, 2, 1), lambda b, h, c: (h, 0, 0)),
                  pl.BlockSpec((None, 8, dk), lambda b, h, c: (b, 0, h)),
                  pl.BlockSpec((CONV_W, dk), lambda b, h, c: (0, h)),
                  pl.BlockSpec((1, dk), lambda b, h, c: (0, h)),
                  pl.BlockSpec((None, dk, dk), lambda b, h, c: (h, 0, 0)),
                  pl.BlockSpec((None, dk, dk), lambda b, h, c: (h, 0, 0)),
                  pl.BlockSpec((None, None, dk, dk), lambda b, h, c: (b, h, 0, 0)),
                  pl.BlockSpec((None, None, 1, dk), lambda b, h, c: (b, h, 0, 0)),
                  pl.BlockSpec((None, None, 1, 128), lambda b, h, c: (b, h, 0, 0)),
                  pl.BlockSpec((1, dk), lambda b, h, c: (0, h)),
                  pl.BlockSpec((1, dk), lambda b, h, c: (0, h))],
        out_specs=[pl.BlockSpec((L, dk), lambda b, h, c: (b * nc + c, h)),
                   pl.BlockSpec((None, None, dk, dk), lambda b, h, c: (b, h, 0, 0)),
                   pl.BlockSpec((None, None, 1, dk), lambda b, h, c: (b, h, 0, 0)),
                   pl.BlockSpec((None, None, 1, 128), lambda b, h, c: (b, h, 0, 0))],
        out_shape=[jax.ShapeDtypeStruct((t_rows, n_head * dk), F32),
                   jax.ShapeDtypeStruct((batch, n_head, dk, dk), F32),
                   jax.ShapeDtypeStruct((batch, n_head, 1, dk), F32),
                   jax.ShapeDtypeStruct((batch, n_head, 1, 128), F32)],
        scratch_shapes=[pltpu.VMEM((L + 8, dk), F32), pltpu.VMEM((dk, dk), F32),
                        pltpu.VMEM((1, dk), F32), pltpu.VMEM((1, 128), F32)],
        compiler_params=_cparams("parallel", "parallel", "arbitrary"),
        name="mlstm",
    )(rows, rows, rows, grow, bias.reshape(n_head, 2, 1), conv0,
      p['conv_w'], p['conv_b'].reshape(1, -1), p['w_q_mlstm'].astype(BF16), p['w_k_mlstm'].astype(BF16),
      c0, n0.reshape(batch, n_head, 1, dk), jnp.broadcast_to(m0[:, :, None, None], (batch, n_head, 1, 128)),
      p['mlstm_norm_g'].reshape(1, -1), p['mlstm_skip'].reshape(1, -1))
    y, c_new, n_new, m_new = outs
    return y, c_new, n_new.reshape(batch, n_head, dk), m_new[:, :, 0, 0]


def _segment_matrix(n_seg, width, order):
    rows = np.zeros((n_seg, n_seg * width), np.float32)
    for i in range(n_seg):
        rows[order[i], i * width:(i + 1) * width] = 1.0
    return jnp.asarray(rows, BF16)


def _segment_scores(seg, kq):
    h1, h2, _ = _split3(kq)
    return _dot_nt(seg, h1) + _dot_nt(seg, h2)


def _key_head_matrices(n_key, n_head, rows):
    col = np.arange(n_key * n_head)
    spread = (col[None, :] // n_head == np.arange(n_key)[:, None]).astype(np.float32)
    own = (col[None, :] % n_head == (np.arange(rows) % n_head)[:, None]).astype(np.float32)
    return jnp.asarray(spread, BF16), jnp.asarray(spread.T, BF16), jnp.asarray(own)


def _spread_dot(a, own, spread, v2):
    return _dot((_dot(a.astype(BF16), spread) * own).astype(BF16), v2)


def _diff_sample_kernel(n_q, n_step, lam_init, pt_ref, q_ref, ka_ref, va_ref, kb_ref, vb_ref, kn_ref, vn_ref,
                        bias_ref, bias_new_ref, seg_ref, own_ref, spread_ref, lam_ref, g_ref, o_ref, m_s, l_s, a_s):
    p = pl.program_id(1)
    n_head = seg_ref.shape[0] // 2
    scale = D_HA ** -0.5

    @pl.when(p == 0)
    def _():
        m_s[...] = jnp.full(m_s.shape, NEG_BIG, F32)
        l_s[...] = jnp.zeros(l_s.shape, F32)
        a_s[...] = jnp.zeros(a_s.shape, F32)

    def scores(k_ref, bias):
        k = k_ref[...]
        sq = [_segment_scores(seg_ref[...], k * q_ref[i:i + 1, :]) for i in range(n_q)]
        return [jnp.concatenate([x[mi * n_head:(mi + 1) * n_head] for x in sq], 0) * scale + bias for mi in range(2)]

    def absorb(s_maps, v_ref):
        v2 = v_ref[...].reshape(-1, v_ref.shape[-1]).astype(BF16)
        for mi, s in enumerate(s_maps):
            m = m_s[mi]
            mn = jnp.maximum(m, jnp.max(s, -1, keepdims=True))
            pr = jnp.exp(s - mn)
            alpha = jnp.exp(m - mn)
            m_s[mi] = mn
            l_s[mi] = alpha * l_s[mi] + jnp.sum(pr, -1, keepdims=True)
            a_s[mi] = alpha * a_s[mi] + _spread_dot(pr, own_ref[...], spread_ref[...], v2)

    @pl.when(p < n_step)
    def _():
        s_a, s_b = scores(ka_ref, bias_ref[0]), scores(kb_ref, bias_ref[1])
        absorb(s_a, va_ref)
        absorb(s_b, vb_ref)

    @pl.when(p == n_step)
    def _():
        absorb(scores(kn_ref, bias_new_ref[...]), vn_ref)
        o_ref[...] = _diff_finish(a_s[0], l_s[0], a_s[1], l_s[1], _lambda(lam_ref, lam_init), g_ref[...], lam_init)


def _diff_attention_sample(q, k_new, v_new, cache_k, cache_v, page_table, rel_bias, lam_vec, g, lam_init):
    db, n_q, width = q.shape
    n_head = width // (2 * D_HA)
    dv = cache_v.shape[-1]
    n_page = page_table.shape[1]
    past = n_page * PAGE
    pad_new = lambda a: jnp.pad(a, ((0, 0), (0, PAGE - n_q), (0, 0)))
    qpos = past + jnp.arange(n_q, dtype=jnp.int32)
    kpos = jnp.arange(past + PAGE, dtype=jnp.int32)
    bias = rel_bias[_t5_bucket(qpos[:, None] - kpos[None, :])]
    bias = jnp.where((kpos[None, :] <= qpos[:, None])[..., None], bias, NEG_BIG)
    bias = jnp.transpose(bias, (0, 2, 1)).reshape(n_q * n_head, n_page + 1, PAGE)
    bias = jnp.transpose(bias, (1, 0, 2))
    seg = _segment_matrix(2 * n_head, D_HA, [(i % 2) * n_head + i // 2 for i in range(2 * n_head)])
    rows = n_q * n_head
    spread, _, own = _key_head_matrices(PAGE, n_head, rows)
    assert n_page % 2 == 0
    n_step = n_page // 2
    page_of = lambda j: (lambda b, p, pt: (pt[b, jnp.minimum(2 * p, n_page - 2) + j], 0, 0))
    page4_of = lambda j: (lambda b, p, pt: (pt[b, jnp.minimum(2 * p, n_page - 2) + j], 0, 0, 0))
    const = lambda b, p, pt: (0, 0)
    grid_spec = pltpu.PrefetchScalarGridSpec(
        num_scalar_prefetch=1,
        grid=(db, n_step + 1),
        in_specs=[pl.BlockSpec((None, n_q, width), lambda b, p, pt: (b, 0, 0)),
                  pl.BlockSpec((None, PAGE, width), page_of(0)),
                  pl.BlockSpec((None, PAGE, n_head, dv), page4_of(0)),
                  pl.BlockSpec((None, PAGE, width), page_of(1)),
                  pl.BlockSpec((None, PAGE, n_head, dv), page4_of(1)),
                  pl.BlockSpec((None, PAGE, width), lambda b, p, pt: (b, 0, 0)),
                  pl.BlockSpec((None, PAGE, n_head, dv), lambda b, p, pt: (b, 0, 0, 0)),
                  pl.BlockSpec((None, 2, rows, PAGE), lambda b, p, pt: (jnp.minimum(p, n_step - 1), 0, 0, 0)),
                  pl.BlockSpec((rows, PAGE), const),
                  pl.BlockSpec(seg.shape, const),
                  pl.BlockSpec(own.shape, const),
                  pl.BlockSpec(spread.shape, const),
                  pl.BlockSpec((4, D_HA), const),
                  pl.BlockSpec((1, dv), const)],
        out_specs=pl.BlockSpec((None, rows, dv), lambda b, p, pt: (b, 0, 0)),
        scratch_shapes=[pltpu.VMEM((2, rows, 1), F32), pltpu.VMEM((2, rows, 1), F32),
                        pltpu.VMEM((2, rows, dv), F32)])
    out = pl.pallas_call(
        functools.partial(_diff_sample_kernel, n_q, n_step, lam_init),
        grid_spec=grid_spec,
        out_shape=jax.ShapeDtypeStruct((db, rows, dv), F32),
        compiler_params=_cparams("parallel", "arbitrary"),
        name="diff_attn_sample",
    )(page_table, q, cache_k, cache_v, cache_k, cache_v, pad_new(k_new),
      pad_new(v_new).reshape(db, PAGE, n_head, dv), bias[:n_page].reshape(n_step, 2, rows, PAGE), bias[n_page],
      seg, own, spread, lam_vec, g.reshape(1, -1))
    return out.reshape(db, n_q, width)


def _sb_sample_kernel(scale, pt_ref, q_ref, ka_ref, va_ref, kb_ref, vb_ref, kn_ref, vn_ref, mask_ref, own_ref,
                      spread_ref, collapse_ref, o_ref, run_s, a_s):
    p = pl.program_id(1)
    keys, _, dh = ka_ref.shape
    suffix = _suffix_matrix(keys)
    q = q_ref[...].astype(BF16)

    def front(k_ref, mask):
        k2 = k_ref[...].reshape(-1, dh).astype(BF16)
        z = _dot_exact_rhs(_dot_nt(q, k2) * own_ref[...], collapse_ref[...], parts=2) * scale
        ls, lk = _log_sigmoid_pair(z)
        if mask is not None:
            lk = lk * mask
        return ls, _dot_exact_rhs(lk, suffix, parts=2), jnp.sum(lk, -1, keepdims=True)

    def absorb(fronts, v_refs, mask):
        run = run_s[...]
        total = a_s[...]
        for (ls, after, mass), v_ref in zip(fronts, v_refs):
            a = jnp.exp(ls + after + run)
            if mask is not None:
                a = a * mask
            total = total + _spread_dot(a, own_ref[...], spread_ref[...], v_ref[...].reshape(-1, dh).astype(BF16))
            run = run + mass
        run_s[...] = run
        a_s[...] = total

    @pl.when(p == 0)
    def _():
        run_s[...] = jnp.zeros(run_s.shape, F32)
        a_s[...] = jnp.zeros(a_s.shape, F32)
        absorb([front(kn_ref, mask_ref[...])], [vn_ref], mask_ref[...])

    @pl.when(p > 0)
    def _():
        absorb([front(ka_ref, None), front(kb_ref, None)], [va_ref, vb_ref], None)

    @pl.when(p == pl.num_programs(1) - 1)
    def _():
        o_ref[...] = a_s[...]


def _sb_attention_sample(q, k_new, v_new, cache_k, cache_v, page_table):
    db, n_q, width = q.shape
    n_head, dh = cache_k.shape[2:]
    n_page = page_table.shape[1]
    pad_new = lambda a: jnp.pad(a, ((0, 0), (0, PAGE - n_q), (0, 0))).reshape(db, PAGE, n_head, dh)
    qi = jnp.repeat(jnp.arange(n_q, dtype=jnp.int32), n_head)[:, None]
    mask = (jnp.arange(PAGE, dtype=jnp.int32)[None, :] < qi).astype(F32)
    rows = n_q * n_head
    spread, collapse, own = _key_head_matrices(PAGE, n_head, rows)
    assert n_page % 2 == 0
    page_of = lambda j: (lambda b, p, pt: (pt[b, n_page - 2 * jnp.maximum(p, 1) + j], 0, 0, 0))
    const = lambda b, p, pt: (0, 0)
    grid_spec = pltpu.PrefetchScalarGridSpec(
        num_scalar_prefetch=1,
        grid=(db, n_page // 2 + 1),
        in_specs=[pl.BlockSpec((None, rows, dh), lambda b, p, pt: (b, 0, 0)),
                  pl.BlockSpec((None, PAGE, n_head, dh), page_of(1)),
                  pl.BlockSpec((None, PAGE, n_head, dh), page_of(1)),
                  pl.BlockSpec((None, PAGE, n_head, dh), page_of(0)),
                  pl.BlockSpec((None, PAGE, n_head, dh), page_of(0)),
                  pl.BlockSpec((None, PAGE, n_head, dh), lambda b, p, pt: (b, 0, 0, 0)),
                  pl.BlockSpec((None, PAGE, n_head, dh), lambda b, p, pt: (b, 0, 0, 0)),
                  pl.BlockSpec(mask.shape, const),
                  pl.BlockSpec(own.shape, const),
                  pl.BlockSpec(spread.shape, const),
                  pl.BlockSpec(collapse.shape, const)],
        out_specs=pl.BlockSpec((None, rows, dh), lambda b, p, pt: (b, 0, 0)),
        scratch_shapes=[pltpu.VMEM((rows, 1), F32), pltpu.VMEM((rows, dh), F32)])
    out = pl.pallas_call(
        functools.partial(_sb_sample_kernel, dh ** -0.5),
        grid_spec=grid_spec,
        out_shape=jax.ShapeDtypeStruct((db, rows, dh), F32),
        compiler_params=_cparams("parallel", "arbitrary"),
        name="sb_attn_sample",
    )(page_table, q.reshape(db, rows, dh), cache_k, cache_v, cache_k, cache_v, pad_new(k_new), pad_new(v_new), mask,
      own, spread, collapse)
    return out.reshape(db, n_q, width)


def _top_k_rows(arrays, k):
    n, lanes = arrays[0].shape
    row = lax.broadcasted_iota(jnp.int32, (n, lanes), 0).astype(F32)
    slot = lax.broadcasted_iota(jnp.int32, (k, lanes), 0)

    def body(i, carry):
        out = []
        for s, vals, ids in carry:
            m = jnp.max(s, 0, keepdims=True)
            first = jnp.min(jnp.where(s == m, row, float(n)), 0, keepdims=True)
            out.append((jnp.where(row == first, -jnp.inf, s), jnp.where(slot == i, m, vals),
                        jnp.where(slot == i, first, ids)))
        return tuple(out)

    zero = jnp.zeros((k, lanes), F32)
    done = lax.fori_loop(0, k, body, tuple((s, zero, zero) for s in arrays))
    return [(vals, ids) for _, vals, ids in done]


def _staircase_rows(a, b, fn, fill):
    k = a.shape[0]
    rows = [fn(a[i:i + 1, :], b[0:k // (i + 1), :]) for i in range(k)]
    n = sum(r.shape[0] for r in rows)
    rows.append(jnp.full((-n % 8, a.shape[1]), fill, F32))
    return jnp.concatenate(rows, 0)


def _peer_route_kernel(wqt_ref, keys_ref, x_ref, idx_ref, g_ref):
    nk = keys_ref.shape[2]
    qt = _dot_nt(wqt_ref[...], x_ref[...].astype(BF16)).astype(BF16)
    half = keys_ref.shape[3]
    for h in range(keys_ref.shape[0]):
        scores = [_dot(keys_ref[h, p], qt[(h * 2 + p) * half:(h * 2 + p + 1) * half, :]) for p in range(2)]
        (s1, i1), (s2, i2) = _top_k_rows(scores, PEER_TOPK)
        cand = _staircase_rows(s1, s2, lambda a, b: a + b, -jnp.inf)
        expert = _staircase_rows(i1, i2, lambda a, b: a * nk + b, 0.0)
        (sc, ci), = _top_k_rows([cand], PEER_TOPK)
        crow = lax.broadcasted_iota(jnp.int32, cand.shape, 0).astype(F32)
        picked = [jnp.sum(jnp.where(crow == ci[j:j + 1, :], expert, 0.0), 0, keepdims=True) for j in range(PEER_TOPK)]
        e = jnp.exp(sc - jnp.max(sc, 0, keepdims=True))
        idx_ref[h] = jnp.concatenate(picked, 0).astype(jnp.int32)
        g_ref[h] = e / jnp.sum(e, 0, keepdims=True)


def _peer_route(x, wqt_bf16, subkeys_bf16):
    t, d = x.shape
    heads = subkeys_bf16.shape[0]
    tt = 128
    return pl.pallas_call(
        _peer_route_kernel,
        grid=(t // tt,),
        in_specs=[pl.BlockSpec(wqt_bf16.shape, lambda i: (0, 0)),
                  pl.BlockSpec(subkeys_bf16.shape, lambda i: (0, 0, 0, 0)),
                  pl.BlockSpec((tt, d), lambda i: (i, 0))],
        out_specs=[pl.BlockSpec((heads, PEER_TOPK, tt), lambda i: (0, 0, i)),
                   pl.BlockSpec((heads, PEER_TOPK, tt), lambda i: (0, 0, i))],
        out_shape=[jax.ShapeDtypeStruct((heads, PEER_TOPK, t), jnp.int32),
                   jax.ShapeDtypeStruct((heads, PEER_TOPK, t), F32)],
        compiler_params=_cparams("parallel"),
        name="peer_route",
    )(wqt_bf16, subkeys_bf16, x)


PEER_TOKEN_CHUNK = 1024
PEER_EXPERT_TILE = 256
PEER_PAIR_BLOCK = 2048


def _gelu(x):
    return 0.5 * x * (1.0 + lax.erf(x * (2.0 ** -0.5)))


def _peer_expert_kernel(rpt, meta_ref, trow_ref, erow_ref, gate_ref, x_ref, u_ref, v_ref, o_ref, stage, act):
    i = pl.program_id(0)
    first, lo, hi = meta_ref[2, i], meta_ref[3, i], meta_ref[4, i]

    @pl.when(first == 1)
    def _():
        o_ref[...] = jnp.zeros(o_ref.shape, F32)

    @pl.when(i == 0)
    def _():
        stage[...] = jnp.zeros(stage.shape, F32)

    @pl.when(hi > lo)
    def _():
        _peer_segment(rpt, trow_ref, erow_ref, gate_ref, x_ref, u_ref, v_ref, o_ref, stage, act, lo, hi)


def _peer_segment(rpt, trow_ref, erow_ref, gate_ref, x_ref, u_ref, v_ref, o_ref, stage, act, lo, hi):
    def rows(ref, j):
        return pl.ds(pl.multiple_of(ref[0, j], rpt), rpt)

    def sweep(one, group, width):
        g_lo, g_hi = (lo + width - 1) // width, hi // width
        head_end = jnp.minimum(g_lo * width, hi)
        tail_start = jnp.maximum(g_hi * width, head_end)
        lax.fori_loop(lo, head_end, lambda j, c: (one(j), c)[1], 0)
        lax.fori_loop(g_lo, g_hi, lambda s, c: (group(s), c)[1], 0)
        lax.fori_loop(tail_start, hi, lambda j, c: (one(j), c)[1], 0)

    def dot_pair(j, row):
        pr = x_ref[rows(trow_ref, j), :] * u_ref[rows(erow_ref, j), :]
        acc = pr[0:8]
        for c in range(1, rpt // 8):
            acc = acc + pr[c * 8:(c + 1) * 8]
        stage[pl.ds(row, 8), :] = acc

    def dot_group(s):
        base = pl.multiple_of(s * 128, 128)
        for r in range(16):
            dot_pair(s * 16 + r, base + r * 8)

    sweep(lambda j: dot_pair(j, pl.multiple_of(j * 8, 8)), dot_group, 16)

    rr = lax.broadcasted_iota(jnp.int32, (128, 128), 0)
    cc = lax.broadcasted_iota(jnp.int32, (128, 128), 1)
    ones = jnp.ones((128, 128), BF16)

    def weights(g):
        base = pl.multiple_of(g * 1024, 1024)
        part = stage[pl.ds(base, 128, stride=8), :]
        for s in range(1, 8):
            part = part + stage[pl.ds(base + s, 128, stride=8), :]
        a = _gelu(_dot_exact_rhs(part, ones, parts=2))
        gate = _dot_exact_rhs(jnp.where(rr == cc, gate_ref[pl.ds(g, 1), :], 0.0), ones, parts=2)
        act[pl.ds(pl.multiple_of(g * 16, 16), 16)] = (a * gate).reshape(16, 8, 128)

    lax.fori_loop(lo // 256, (hi + 255) // 256, lambda t, c: (weights(2 * t), weights(2 * t + 1), c)[2], 0)

    def mix_one(j):
        out = rows(trow_ref, j)
        o_ref[out, :] = o_ref[out, :] + v_ref[rows(erow_ref, j), :] * act[j // 8, pl.ds(j % 8, 1), :]

    def mix_group(s):
        toks, news = [], []
        acc = None
        for r in range(8):
            j = s * 8 + r
            tok = trow_ref[0, j]
            c = v_ref[rows(erow_ref, j), :] * act[s, r:r + 1, :]
            acc = c if r == 0 else jnp.where(tok == toks[-1], acc, 0.0) + c
            toks.append(tok)
            news.append(o_ref[pl.ds(pl.multiple_of(tok, rpt), rpt), :] + acc)
        for tok, new in zip(toks, news):
            o_ref[pl.ds(pl.multiple_of(tok, rpt), rpt), :] = new

    sweep(mix_one, mix_group, 8)


def _peer_plan(idx, gates, t_pad, rpt):
    heads, topk, t = idx.shape
    pb = PEER_PAIR_BLOCK
    e_bits = PEER_EXPERT_TILE.bit_length() - 1
    c_bits = e_bits + PEER_TOKEN_CHUNK.bit_length() - 1
    n_tile = (PEER_NK * PEER_NK) // PEER_EXPERT_TILE
    n_cell = (t_pad // PEER_TOKEN_CHUNK) * n_tile
    n_chunk = t_pad // PEER_TOKEN_CHUNK
    per_chunk = PEER_TOKEN_CHUNK * heads * topk
    assert per_chunk % pb == 0 and ((n_cell + 1) << c_bits) < 2 ** 31
    n_blk = n_chunk * per_chunk // pb
    e = jnp.transpose(idx, (2, 0, 1)).reshape(-1)
    g = jnp.transpose(gates, (2, 0, 1)).reshape(-1)
    tok = jnp.repeat(jnp.arange(t, dtype=jnp.int32), heads * topk)
    cell = (tok // PEER_TOKEN_CHUNK) * n_tile + e // PEER_EXPERT_TILE
    key = (cell << c_bits) | ((tok % PEER_TOKEN_CHUNK) << e_bits) | (e % PEER_EXPERT_TILE)
    fill = n_chunk * per_chunk - key.shape[0]
    key = jnp.pad(key, (0, fill), constant_values=n_cell << c_bits).reshape(n_chunk, per_chunk)
    key_s, g_s = lax.sort((key, jnp.pad(g, (0, fill)).reshape(n_chunk, per_chunk)), dimension=1, num_keys=1,
                          is_stable=False)
    key_s, g_s = key_s.reshape(-1), g_s.reshape(-1)
    bounds = jnp.searchsorted(key_s, jnp.arange(n_cell + 1, dtype=jnp.int32) << c_bits, side='left').astype(jnp.int32)
    first_blk = bounds[:-1] // pb
    n_seg_cell = jnp.where(bounds[1:] > bounds[:-1], (bounds[1:] - 1) // pb - first_blk + 1, 0)
    seg_end = jnp.cumsum(n_seg_cell)
    step = jnp.arange(n_blk + n_cell, dtype=jnp.int32)
    last = seg_end[-1] - 1
    step_c = jnp.minimum(step, last)
    q = jnp.minimum(jnp.searchsorted(seg_end, step_c, side='right'), n_cell - 1).astype(jnp.int32)
    blk = first_blk[q] + step_c - (seg_end - n_seg_cell)[q]
    lo = jnp.where(step <= last, jnp.clip(bounds[q] - blk * pb, 0, pb), 0)
    hi = jnp.where(step <= last, jnp.clip(bounds[q + 1] - blk * pb, 0, pb), 0)
    chunk = q // n_tile
    first = jnp.concatenate([jnp.ones((1,), jnp.int32), (chunk[1:] != chunk[:-1]).astype(jnp.int32)])
    meta = jnp.stack([chunk, q % n_tile, first, lo, hi, blk]).astype(jnp.int32)
    trow = ((key_s >> e_bits) & (PEER_TOKEN_CHUNK - 1)) * rpt
    erow = (key_s & (PEER_EXPERT_TILE - 1)) * rpt
    return meta, trow.reshape(n_blk, 1, pb), erow.reshape(n_blk, 1, pb), g_s.reshape(n_blk, pb // 128, 128)


def _peer_experts(x, idx, gates, u, v):
    t, d = x.shape
    t_pad = -(-t // PEER_TOKEN_CHUNK) * PEER_TOKEN_CHUNK
    rpt = d // 128
    meta, trow, erow, gate = _peer_plan(idx, gates, t_pad, rpt)
    rows = lambda a: a.reshape(a.shape[0] * rpt, 128)
    pairs = pl.BlockSpec((None, 1, PEER_PAIR_BLOCK), lambda i, m: (m[5, i], 0, 0), memory_space=pltpu.SMEM)
    tokens = pl.BlockSpec((PEER_TOKEN_CHUNK * rpt, 128), lambda i, m: (m[0, i], 0))
    experts = pl.BlockSpec((PEER_EXPERT_TILE * rpt, 128), lambda i, m: (m[1, i], 0))
    grid_spec = pltpu.PrefetchScalarGridSpec(
        num_scalar_prefetch=1,
        grid=(meta.shape[1],),
        in_specs=[pairs, pairs,
                  pl.BlockSpec((None, PEER_PAIR_BLOCK // 128, 128), lambda i, m: (m[5, i], 0, 0)),
                  tokens, experts, experts],
        out_specs=tokens,
        scratch_shapes=[pltpu.VMEM((PEER_PAIR_BLOCK * 8, 128), F32),
                        pltpu.VMEM((PEER_PAIR_BLOCK // 8, 8, 128), F32)])
    out = pl.pallas_call(
        functools.partial(_peer_expert_kernel, rpt),
        grid_spec=grid_spec,
        out_shape=jax.ShapeDtypeStruct((t_pad * rpt, 128), F32),
        compiler_params=pltpu.CompilerParams(dimension_semantics=("arbitrary",),
                                             vmem_limit_bytes=56 * 1024 * 1024),
        name="peer_experts",
    )(meta, trow, erow, gate, rows(jnp.pad(x, ((0, t_pad - t), (0, 0)))), rows(u), rows(v))
    return out.reshape(t_pad, d)[:t]


def _peer_layer(x, wq, subkeys, u, v, g, b, alpha):
    idx, gates = _peer_route(x, wq.T.astype(BF16), subkeys.astype(BF16))
    return _res_ln(x, _peer_experts(x, idx, gates, u, v), g, b, alpha)


def _pad_cols(w, mult):
    return jnp.pad(w, ((0, 0), (0, -w.shape[1] % mult)))


def kernel(x_prompt, x_sample, cache_diff_k, cache_diff_v, state_mlstm_C, state_mlstm_n, state_mlstm_m, state_mlstm_conv, cache_sb_k, cache_sb_v, page_table, rel_bias, w_in_even, b_ig, b_fg, lam_q1, lam_k1, lam_q2, lam_k2, diff_norm_g, conv_w, conv_b, w_q_mlstm, w_k_mlstm, mlstm_norm_g, mlstm_skip, w_out_even, w_qkv_odd, w_out_odd, ln_g, ln_b, peer_wq, peer_subkeys, peer_u, peer_v):
    nb, seq, d = x_prompt.shape
    db, n_q, _ = x_sample.shape
    tp, ts = nb * seq, db * n_q
    n_pool = cache_diff_k.shape[1]
    depth = ln_g.shape[0]
    alpha = (2.0 * depth) ** 0.25
    h_a, h_b, h_c = d // 256, w_q_mlstm.shape[1], d // 128
    w_a, w_b = h_a * 2 * D_HA, d // 2
    assert depth == 2 and w_in_even.shape[0] == 1 and w_qkv_odd.shape[0] == 1 and w_a == w_b
    lam_init = 0.8 - 0.6 * math.exp(-0.3 * 0)
    x = jnp.concatenate([x_prompt.reshape(tp, d), x_sample.reshape(ts, d)])

    n_main = 3 * w_a + 3 * w_b
    proj = _matmul(x, _pad_cols(w_in_even[0], 896).astype(BF16), 896)
    qa, ka, va, u_b = (proj[:, i * w_a:(i + 1) * w_a] for i in range(4))
    gates = proj[:, n_main:n_main + 2 * h_b]
    lam_vec = jnp.stack([lam_q1[0], lam_k1[0], lam_q2[0], lam_k2[0]])
    ya_p = _diff_attention_prompt(proj, nb, seq, h_a, rel_bias, lam_vec, diff_norm_g[0], lam_init, 256)
    smp = lambda a: a[tp:].reshape(db, n_q, -1)
    ya_s = _diff_attention_sample(smp(qa), smp(ka), smp(va), cache_diff_k[0].reshape(n_pool, PAGE, w_a),
                                  cache_diff_v[0], page_table, rel_bias, lam_vec,
                                  diff_norm_g[0], lam_init)
    mp = {'b_ig': b_ig[0], 'b_fg': b_fg[0], 'conv_w': conv_w[0], 'conv_b': conv_b[0], 'w_q_mlstm': w_q_mlstm[0],
          'w_k_mlstm': w_k_mlstm[0], 'mlstm_norm_g': mlstm_norm_g[0], 'mlstm_skip': mlstm_skip[0]}
    lane0 = 3 * w_a // 256
    yb_p, c_p, n_p, m_p = _mlstm(proj, gates[:tp], nb, seq, MLSTM_CHUNK, h_b, lane0,
                                 jnp.zeros((nb, 8, w_b), F32), jnp.zeros((nb, h_b, 256, 256), F32),
                                 jnp.zeros((nb, h_b, 256), F32), jnp.zeros((nb, h_b), F32), mp)
    s_pad = 8
    pad_t = lambda a, val=0.0: jnp.pad(a, ((0, 0), (0, s_pad - n_q), (0, 0)), constant_values=val)
    gate_s = smp(gates)
    gate_s = jnp.concatenate([pad_t(gate_s[..., :h_b], NEG_BIG), pad_t(gate_s[..., h_b:], -NEG_BIG)], -1)
    conv0 = jnp.pad(state_mlstm_conv[0], ((0, 0), (8 - (CONV_W - 1), 0), (0, 0)))
    yb_s, c_s, n_s, m_s = _mlstm(pad_t(smp(proj)).reshape(db * s_pad, -1), gate_s.reshape(db * s_pad, -1), db, s_pad,
                                 s_pad, h_b, lane0, conv0, state_mlstm_C[0], state_mlstm_n[0], state_mlstm_m[0], mp)
    yb_s = yb_s.reshape(db, s_pad, w_b)[:, :n_q].reshape(ts, w_b)
    y = jnp.concatenate([jnp.concatenate([ya_p, yb_p], -1), jnp.concatenate([ya_s.reshape(ts, w_a), yb_s], -1)])
    x = _matmul_res_ln(y, w_out_even[0].astype(BF16), x, ln_g[0, 0], ln_b[0, 0], alpha)
    x = _peer_layer(x, peer_wq[0], peer_subkeys[0], peer_u[0], peer_v[0], ln_g[0, 1], ln_b[0, 1], alpha)

    qkv = _matmul(x, w_qkv_odd[0].astype(BF16), 1024)
    y_p = _sb_attention_prompt(qkv, nb, seq, h_c, 128, 256)
    q_s, k_s, v_s = (smp(qkv[:, i * d:(i + 1) * d]) for i in range(3))
    y_s = _sb_attention_sample(q_s, k_s, v_s, cache_sb_k[0], cache_sb_v[0], page_table)
    y = jnp.concatenate([y_p, y_s.reshape(ts, d)])
    x = _matmul_res_ln(y, w_out_odd[0].astype(BF16), x, ln_g[1, 0], ln_b[1, 0], alpha)
    x = _peer_layer(x, peer_wq[1], peer_subkeys[1], peer_u[1], peer_v[1], ln_g[1, 1], ln_b[1, 1], alpha)

    prm = lambda a, *s: a[:tp].reshape((1, nb, seq) + s)
    u_p = proj[:tp, 3 * w_a:3 * w_a + w_b].reshape(nb, seq, w_b)
    u_s = smp(u_b)
    sbk, sbv = qkv[:, d:2 * d], qkv[:, 2 * d:]
    return (x[:tp].reshape(nb, seq, d), x[tp:].reshape(db, n_q, d),
            prm(ka, h_a, 2, D_HA), prm(va, h_a, 2 * D_HA),
            smp(ka).reshape(1, db, n_q, h_a, 2, D_HA), smp(va).reshape(1, db, n_q, h_a, 2 * D_HA),
            c_p[None], n_p[None], m_p[None], u_p[None, :, seq - (CONV_W - 1):],
            c_s[None], n_s[None], m_s[None], u_s[None, :, n_q - (CONV_W - 1):],
            prm(sbk, h_c, 128), prm(sbv, h_c, 128),
            smp(sbk).reshape(1, db, n_q, h_c, 128), smp(sbv).reshape(1, db, n_q, h_c, 128))
```

```python
import functools
import math

import jax
import jax.numpy as jnp
import numpy as np
from jax import lax
from jax.experimental import pallas as pl
from jax.experimental.pallas import tpu as pltpu

F32 = jnp.float32
BF16 = jnp.bfloat16

LN_EPS = 1e-5
NEG_BIG = -1e30
VMEM_LIMIT = 48 * 1024 * 1024

D_HA = 64
N_BUCKETS = 32
MAX_DIST = 128
CONV_W = 4
MLSTM_CHUNK = 128
PAGE = 128
PEER_HEADS = 8
PEER_NK = 128
PEER_TOPK = 16


def _cparams(*sem):
    return pltpu.CompilerParams(dimension_semantics=sem, vmem_limit_bytes=VMEM_LIMIT)


def _dot(a, b):
    return jnp.dot(a, b, preferred_element_type=F32)


def _dot_nt(a, b):
    return lax.dot_general(a, b, (((1,), (1,)), ((), ())), preferred_element_type=F32)


def _dot_tn(a, b):
    return lax.dot_general(a, b, (((0,), (0,)), ((), ())), preferred_element_type=F32)


def _split3(x):
    h1 = x.astype(BF16)
    r1 = x - h1.astype(F32)
    h2 = r1.astype(BF16)
    h3 = (r1 - h2.astype(F32)).astype(BF16)
    return h1, h2, h3


def _dot_exact_rhs(x, m_bf16, parts=3):
    return sum(_dot(h, m_bf16) for h in _split3(x)[:parts])


def _row_tile(m):
    for t in (384, 256, 128, 64, 32, 16, 8):
        if m % t == 0:
            return t
    raise ValueError(f"row count {m} is not a multiple of 8")


def _mm_kernel(x_ref, w_ref, o_ref):
    o_ref[...] = _dot(x_ref[...].astype(BF16), w_ref[...])


def _matmul(x, w_bf16, tn):
    m, k = x.shape
    n = w_bf16.shape[1]
    tm = _row_tile(m)
    return pl.pallas_call(
        _mm_kernel,
        grid=(m // tm, n // tn),
        in_specs=[pl.BlockSpec((tm, k), lambda i, j: (i, 0)),
                  pl.BlockSpec((k, tn), lambda i, j: (0, j))],
        out_specs=pl.BlockSpec((tm, tn), lambda i, j: (i, j)),
        out_shape=jax.ShapeDtypeStruct((m, n), F32),
        compiler_params=_cparams("parallel", "arbitrary"),
        name="matmul",
    )(x, w_bf16)


def _mm_nt_kernel(w_ref, x_ref, o_ref):
    o_ref[...] = _dot_nt(w_ref[...], x_ref[...].astype(BF16))


def _matmul_t(wt_bf16, x):
    n, k = wt_bf16.shape
    m = x.shape[0]
    tm = _row_tile(m)
    return pl.pallas_call(
        _mm_nt_kernel,
        grid=(m // tm,),
        in_specs=[pl.BlockSpec((n, k), lambda i: (0, 0)),
                  pl.BlockSpec((tm, k), lambda i: (i, 0))],
        out_specs=pl.BlockSpec((n, tm), lambda i: (0, i)),
        out_shape=jax.ShapeDtypeStruct((n, m), F32),
        compiler_params=_cparams("parallel"),
        name="matmul_t",
    )(wt_bf16, x)


def _layer_norm(xf, g, b):
    mu = jnp.mean(xf, -1, keepdims=True)
    xc = xf - mu
    var = jnp.mean(xc * xc, -1, keepdims=True)
    return xc * lax.rsqrt(var + LN_EPS) * g + b


def _mm_res_ln_kernel(alpha, a_ref, w_ref, x_ref, g_ref, b_ref, o_ref):
    y = _dot(a_ref[...].astype(BF16), w_ref[...])
    o_ref[...] = _layer_norm(alpha * x_ref[...] + y, g_ref[...], b_ref[...])


def _matmul_res_ln(a, w_bf16, x, g, b, alpha):
    m, k = a.shape
    n = w_bf16.shape[1]
    tm = _row_tile(m)
    return pl.pallas_call(
        functools.partial(_mm_res_ln_kernel, alpha),
        grid=(m // tm,),
        in_specs=[pl.BlockSpec((tm, k), lambda i: (i, 0)),
                  pl.BlockSpec((k, n), lambda i: (0, 0)),
                  pl.BlockSpec((tm, n), lambda i: (i, 0)),
                  pl.BlockSpec((1, n), lambda i: (0, 0)),
                  pl.BlockSpec((1, n), lambda i: (0, 0))],
        out_specs=pl.BlockSpec((tm, n), lambda i: (i, 0)),
        out_shape=jax.ShapeDtypeStruct((m, n), F32),
        compiler_params=_cparams("parallel"),
        name="matmul_res_ln",
    )(a, w_bf16, x, g.reshape(1, n), b.reshape(1, n))


def _res_ln_kernel(alpha, x_ref, f_ref, g_ref, b_ref, o_ref):
    o_ref[...] = _layer_norm(alpha * x_ref[...] + f_ref[...], g_ref[...], b_ref[...])


def _res_ln(x, f, g, b, alpha):
    m, n = x.shape
    tm = _row_tile(m)
    return pl.pallas_call(
        functools.partial(_res_ln_kernel, alpha),
        grid=(m // tm,),
        in_specs=[pl.BlockSpec((tm, n), lambda i: (i, 0)),
                  pl.BlockSpec((tm, n), lambda i: (i, 0)),
                  pl.BlockSpec((1, n), lambda i: (0, 0)),
                  pl.BlockSpec((1, n), lambda i: (0, 0))],
        out_specs=pl.BlockSpec((tm, n), lambda i: (i, 0)),
        out_shape=jax.ShapeDtypeStruct((m, n), F32),
        compiler_params=_cparams("parallel"),
        name="res_ln",
    )(x, f, g.reshape(1, n), b.reshape(1, n))


def _t5_bucket(dist):
    n = jnp.maximum(dist, 0)
    exact = N_BUCKETS // 2
    large = exact + (jnp.log(jnp.maximum(n, exact).astype(F32) / exact)
                     / math.log(MAX_DIST / exact) * (N_BUCKETS - exact)).astype(jnp.int32)
    return jnp.where(n < exact, n, jnp.minimum(large, N_BUCKETS - 1))


def _lambda(lam_ref, lam_init):
    lq1, lk1, lq2, lk2 = lam_ref[0:1, :], lam_ref[1:2, :], lam_ref[2:3, :], lam_ref[3:4, :]
    return (jnp.exp(jnp.sum(lq1 * lk1, keepdims=True)) - jnp.exp(jnp.sum(lq2 * lk2, keepdims=True))
            + lam_init)


def _diff_finish(acc1, l1, acc2, l2, lam, g, lam_init):
    o = acc1 / l1 - lam * (acc2 / l2)
    ms = jnp.mean(o * o, -1, keepdims=True)
    return o * lax.rsqrt(ms + LN_EPS) * g * (1.0 - lam_init)


def _diff_prompt_kernel(tq, lam_init, q_ref, k_ref, v_ref, bias_ref, far_ref, lam_ref, g_ref, o_ref):
    qi = pl.program_id(2)
    scale = D_HA ** -0.5
    q = q_ref[...]
    lane = lax.broadcasted_iota(jnp.int32, q.shape, 1)
    q1 = jnp.where(lane < D_HA, q, 0.0).astype(BF16)
    q2 = jnp.where(lane >= D_HA, q, 0.0).astype(BF16)
    dv = v_ref.shape[-1]

    def update(state, ki, bias, causal):
        m1, l1, a1, m2, l2, a2 = state
        start = pl.multiple_of(ki * tq, tq)
        kb = k_ref[pl.ds(start, tq), :].astype(BF16)
        vb = v_ref[pl.ds(start, tq), :].astype(BF16)
        out = []
        for qm, m, l, a in ((q1, m1, l1, a1), (q2, m2, l2, a2)):
            s = _dot_nt(qm, kb) * scale + bias
            if causal:
                r = lax.broadcasted_iota(jnp.int32, s.shape, 0)
                c = lax.broadcasted_iota(jnp.int32, s.shape, 1)
                s = jnp.where(c <= r, s, NEG_BIG)
            mn = jnp.maximum(m, jnp.max(s, -1, keepdims=True))
            p = jnp.exp(s - mn)
            alpha = jnp.exp(m - mn)
            out += [mn, alpha * l + jnp.sum(p, -1, keepdims=True), alpha * a + _dot(p.astype(BF16), vb)]
        return tuple(out)

    init = (jnp.full((tq, 1), NEG_BIG, F32), jnp.zeros((tq, 1), F32), jnp.zeros((tq, dv), F32)) * 2
    state = update(init, qi, bias_ref[0], True)
    sub = jnp.where(qi >= 1, bias_ref[1], NEG_BIG)
    state = update(state, jnp.maximum(qi - 1, 0), sub, False)
    far = far_ref[...]
    n_far = jnp.maximum(qi - 1, 0)
    state = lax.fori_loop(0, n_far // 2,
                          lambda j, st: update(update(st, 2 * j, far, False), 2 * j + 1, far, False), state)
    m1, l1, a1, m2, l2, a2 = lax.fori_loop(0, n_far % 2, lambda j, st: update(st, n_far - 1, far, False), state)
    o_ref[...] = _diff_finish(a1, l1, a2, l2, _lambda(lam_ref, lam_init), g_ref[...], lam_init)


def _diff_bias_tables(rel_bias, tq):
    h = rel_bias.shape[1]
    r = jnp.arange(tq, dtype=jnp.int32)[:, None]
    c = jnp.arange(tq, dtype=jnp.int32)[None, :]
    tiles = jnp.stack([rel_bias[_t5_bucket(r - c)], rel_bias[_t5_bucket(r - c + tq)]])
    return jnp.transpose(tiles, (3, 0, 1, 2)), rel_bias[N_BUCKETS - 1].reshape(h, 1, 1)


def _diff_attention_prompt(proj, batch, seq, n_head, rel_bias, lam_vec, g, lam_init, tq):
    assert tq >= MAX_DIST and seq % tq == 0
    dh = 2 * D_HA
    nq = seq // tq
    bias_tiles, far = _diff_bias_tables(rel_bias, tq)
    return pl.pallas_call(
        functools.partial(_diff_prompt_kernel, tq, lam_init),
        grid=(batch, n_head, nq),
        in_specs=[pl.BlockSpec((tq, dh), lambda b, h, i: (b * nq + i, h)),
                  pl.BlockSpec((seq, dh), lambda b, h, i: (b, n_head + h)),
                  pl.BlockSpec((seq, dh), lambda b, h, i: (b, 2 * n_head + h)),
                  pl.BlockSpec((None, 2, tq, tq), lambda b, h, i: (h, 0, 0, 0)),
                  pl.BlockSpec((None, 1, 1), lambda b, h, i: (h, 0, 0)),
                  pl.BlockSpec((4, D_HA), lambda b, h, i: (0, 0)),
                  pl.BlockSpec((1, dh), lambda b, h, i: (0, 0))],
        out_specs=pl.BlockSpec((tq, dh), lambda b, h, i: (b * nq + i, h)),
        out_shape=jax.ShapeDtypeStruct((batch * seq, n_head * dh), F32),
        compiler_params=_cparams("parallel", "parallel", "arbitrary"),
        name="diff_attn_prompt",
    )(proj, proj, proj, bias_tiles, far, lam_vec, g.reshape(1, dh))


def _log_sigmoid_pair(z):
    t = jnp.log(1.0 + jnp.exp(-jnp.abs(z)))
    return jnp.minimum(z, 0.0) - t, jnp.minimum(-z, 0.0) - t


def _suffix_matrix(n):
    j = lax.broadcasted_iota(jnp.int32, (n, n), 0)
    s = lax.broadcasted_iota(jnp.int32, (n, n), 1)
    return jnp.where(j > s, 1.0, 0.0).astype(BF16)


def _sb_prompt_kernel(tq, scale, q_ref, k_ref, v_ref, o_ref):
    qi = pl.program_id(2)
    q = q_ref[...].astype(BF16)
    dv = v_ref.shape[-1]
    suffix = _suffix_matrix(tq)

    def update(state, ki, causal):
        run, acc = state
        start = pl.multiple_of(ki * tq, tq)
        kb = k_ref[pl.ds(start, tq), :].astype(BF16)
        vb = v_ref[pl.ds(start, tq), :].astype(BF16)
        z = _dot_nt(q, kb) * scale
        ls, lk = _log_sigmoid_pair(z)
        if causal:
            r = lax.broadcasted_iota(jnp.int32, z.shape, 0)
            c = lax.broadcasted_iota(jnp.int32, z.shape, 1)
            mask = c < r
            lk = jnp.where(mask, lk, 0.0)
        after = _dot_exact_rhs(lk, suffix, parts=2) + run
        a = jnp.exp(ls + after)
        if causal:
            a = jnp.where(mask, a, 0.0)
        return run + jnp.sum(lk, -1, keepdims=True), acc + _dot(a.astype(BF16), vb)

    state = update((jnp.zeros((tq, 1), F32), jnp.zeros((tq, dv), F32)), qi, True)
    state = lax.fori_loop(0, qi // 2,
                          lambda j, st: update(update(st, qi - 1 - 2 * j, False), qi - 2 - 2 * j, False), state)
    _, acc = lax.fori_loop(0, qi % 2, lambda j, st: update(st, 0, False), state)
    o_ref[...] = acc


def _sb_attention_prompt(qkv, batch, seq, n_head, dh, tq):
    nq = seq // tq
    return pl.pallas_call(
        functools.partial(_sb_prompt_kernel, tq, dh ** -0.5),
        grid=(batch, n_head, nq),
        in_specs=[pl.BlockSpec((tq, dh), lambda b, h, i: (b * nq + i, h)),
                  pl.BlockSpec((seq, dh), lambda b, h, i: (b, n_head + h)),
                  pl.BlockSpec((seq, dh), lambda b, h, i: (b, 2 * n_head + h))],
        out_specs=pl.BlockSpec((tq, dh), lambda b, h, i: (b * nq + i, h)),
        out_shape=jax.ShapeDtypeStruct((batch * seq, n_head * dh), F32),
        compiler_params=_cparams("parallel", "parallel", "arbitrary"),
        name="sb_attn_prompt",
    )(qkv, qkv, qkv)


def _mlstm_kernel(L, u_ref, v_ref, ob_ref, grow_ref, bcol_ref, conv0_ref, cw_ref, cb_ref,
                  wq_ref, wk_ref, c0_ref, n0_ref, m0_ref, g_ref, skip_ref,
                  y_ref, c_out, n_out, m_out, ext, c_s, n_s, m_s):
    c = pl.program_id(2)
    dk = u_ref.shape[-1]

    @pl.when(c == 0)
    def _():
        ext[0:8, :] = conv0_ref[...]
        c_s[...] = c0_ref[...]
        n_s[...] = n0_ref[...]
        m_s[...] = m0_ref[...]

    ext[8:8 + L, :] = u_ref[...]
    uc = cb_ref[...]
    for j in range(CONV_W):
        uc = uc + ext[pl.ds(8 - (CONV_W - 1) + j, L), :] * cw_ref[j:j + 1, :]
    ext[0:8, :] = ext[L:L + 8, :]
    ua = uc * jax.nn.sigmoid(uc)
    uab = ua.astype(BF16)
    qf = _dot(uab, wq_ref[...])
    q = qf.astype(BF16)
    kf = _dot(uab, wk_ref[...]) * dk ** -0.5
    k = kf.astype(BF16)
    vb = v_ref[...].astype(BF16)

    gr = grow_ref[...] + bcol_ref[...]
    li_r, lf_r = gr[0:1, :], _log_sigmoid_pair(gr[1:2, :])[0]
    t = lax.broadcasted_iota(jnp.int32, (L, L), 0)
    s = lax.broadcasted_iota(jnp.int32, (L, L), 1)
    causal = s <= t
    li_c = jnp.sum(jnp.where(s == t, li_r, 0.0), -1, keepdims=True)
    b_c = _dot_exact_rhs(jnp.where(causal, lf_r, 0.0), jnp.ones((L, 8), BF16))[:, 0:1]
    b_r = _dot_exact_rhs(jnp.broadcast_to(lf_r, (8, L)), jnp.where(t <= s, 1.0, 0.0).astype(BF16))[0:1, :]

    m = m_s[:, 0:1]
    cmat = c_s[...]
    nrow = n_s[...]
    d = jnp.where(causal, b_c - b_r + li_r, NEG_BIG)
    inter = b_c + m
    mt = jnp.maximum(inter, jnp.max(d, -1, keepdims=True))
    w = jnp.exp(d - mt) * _dot_nt(q, k)
    sc = jnp.exp(inter - mt)
    num = _dot(w.astype(BF16), vb) + sc * _dot(q, cmat.astype(BF16))
    den = jnp.sum(w, -1, keepdims=True) + sc * jnp.sum(qf * nrow, -1, keepdims=True)
    h = num / jnp.maximum(jnp.abs(den), jnp.exp(-mt))

    m_new = mt[L - 1:L, :]
    b_last = b_c[L - 1:L, :]
    wl = jnp.exp(b_last - b_c + li_c - m_new)
    dec = jnp.exp(b_last + m - m_new)
    kw = kf * wl
    c_s[...] = dec * cmat + _dot_tn(kw.astype(BF16), vb)
    n_s[...] = dec * nrow + jnp.sum(kw, 0, keepdims=True)
    m_s[...] = jnp.broadcast_to(m_new, m_s.shape)

    mu = jnp.mean(h, -1, keepdims=True)
    hc = h - mu
    var = jnp.mean(hc * hc, -1, keepdims=True)
    hn = hc * lax.rsqrt(var + LN_EPS) * g_ref[...]
    y_ref[...] = jax.nn.sigmoid(ob_ref[...]) * (hn + skip_ref[...] * ua)

    @pl.when(c == pl.num_programs(2) - 1)
    def _():
        c_out[...] = c_s[...]
        n_out[...] = n_s[...]
        m_out[...] = m_s[...]


def _mlstm(rows, gates, batch, seq, L, n_head, lane0, conv0, c0, n0, m0, p):
    dk = 256
    nc = seq // L
    t_rows = batch * seq
    grow = jnp.transpose(gates.reshape(batch * nc, L, 2, n_head), (3, 0, 2, 1))
    bias = jnp.stack([p['b_ig'], p['b_fg']], -1)
    outs = pl.pallas_call(
        functools.partial(_mlstm_kernel, L),
        grid=(batch, n_head, nc),
        in_specs=[pl.BlockSpec((L, dk), lambda b, h, c: (b * nc + c, lane0 + h)),
                  pl.BlockSpec((L, dk), lambda b, h, c: (b * nc + c, lane0 + n_head + h)),
                  pl.BlockSpec((L, dk), lambda b, h, c: (b * nc + c, lane0 + 2 * n_head + h)),
                  pl.BlockSpec((None, None, 2, L), lambda b, h, c: (h, b * nc + c, 0, 0)),
                  pl.BlockSpec((None, 2, 1), lambda b, h, c: (h, 0, 0)),
                  pl.BlockSpec((None, 8, dk), lambda b, h, c: (b, 0, h)),
                  pl.BlockSpec((CONV_W, dk), lambda b, h, c: (0, h)),
                  pl.BlockSpec((1, dk), lambda b, h, c: (0, h)),
                  pl.BlockSpec((None, dk, dk), lambda b, h, c: (h, 0, 0)),
                  pl.BlockSpec((None, dk, dk), lambda b, h, c: (h, 0, 0)),
                  pl.BlockSpec((None, None, dk, dk), lambda b, h, c: (b, h, 0, 0)),
                  pl.BlockSpec((None, None, 1, dk), lambda b, h, c: (b, h, 0, 0)),
                  pl.BlockSpec((None, None, 1, 128), lambda b, h, c: (b, h, 0, 0)),
                  pl.BlockSpec((1, dk), lambda b, h, c: (0, h)),
                  pl.BlockSpec((1, dk), lambda b, h, c: (0, h))],
        out_specs=[pl.BlockSpec((L, dk), lambda b, h, c: (b * nc + c, h)),
                   pl.BlockSpec((None, None, dk, dk), lambda b, h, c: (b, h, 0, 0)),
                   pl.BlockSpec((None, None, 1, dk), lambda b, h, c: (b, h, 0, 0)),
                   pl.BlockSpec((None, None, 1, 128), lambda b, h, c: (b, h, 0, 0))],
        out_shape=[jax.ShapeDtypeStruct((t_rows, n_head * dk), F32),
                   jax.ShapeDtypeStruct((batch, n_head, dk, dk), F32),
                   jax.ShapeDtypeStruct((batch, n_head, 1, dk), F32),
                   jax.ShapeDtypeStruct((batch, n_head, 1, 128), F32)],
        scratch_shapes=[pltpu.VMEM((L + 8, dk), F32), pltpu.VMEM((dk, dk), F32),
                        pltpu.VMEM((1, dk), F32), pltpu.VMEM((1, 128), F32)],
        compiler_params=_cparams("parallel", "parallel", "arbitrary"),
        name="mlstm",
    )(rows, rows, rows, grow, bias.reshape(n_head, 2, 1), conv0,
      p['conv_w'], p['conv_b'].reshape(1, -1), p['w_q_mlstm'].astype(BF16), p['w_k_mlstm'].astype(BF16),
      c0, n0.reshape(batch, n_head, 1, dk), jnp.broadcast_to(m0[:, :, None, None], (batch, n_head, 1, 128)),
      p['mlstm_norm_g'].reshape(1, -1), p['mlstm_skip'].reshape(1, -1))
    y, c_new, n_new, m_new = outs
    return y, c_new, n_new.reshape(batch, n_head, dk), m_new[:, :, 0, 0]


def _segment_matrix(n_seg, width, order):
    rows = np.zeros((n_seg, n_seg * width), np.float32)
    for i in range(n_seg):
        rows[order[i], i * width:(i + 1) * width] = 1.0
    return jnp.asarray(rows, BF16)


def _segment_scores(seg, kq):
    h1, h2, _ = _split3(kq)
    return _dot_nt(seg, h1) + _dot_nt(seg, h2)


def _key_head_matrices(n_key, n_head, rows):
    col = np.arange(n_key * n_head)
    spread = (col[None, :] // n_head == np.arange(n_key)[:, None]).astype(np.float32)
    own = (col[None, :] % n_head == (np.arange(rows) % n_head)[:, None]).astype(np.float32)
    return jnp.asarray(spread, BF16), jnp.asarray(spread.T, BF16), jnp.asarray(own)


def _spread_dot(a, own, spread, v2):
    return _dot((_dot(a.astype(BF16), spread) * own).astype(BF16), v2)


def _diff_sample_kernel(n_q, n_step, lam_init, pt_ref, q_ref, ka_ref, va_ref, kb_ref, vb_ref, kn_ref, vn_ref,
                        bias_ref, bias_new_ref, seg_ref, own_ref, spread_ref, lam_ref, g_ref, o_ref, m_s, l_s, a_s):
    p = pl.program_id(1)
    n_head = seg_ref.shape[0] // 2
    scale = D_HA ** -0.5

    @pl.when(p == 0)
    def _():
        m_s[...] = jnp.full(m_s.shape, NEG_BIG, F32)
        l_s[...] = jnp.zeros(l_s.shape, F32)
        a_s[...] = jnp.zeros(a_s.shape, F32)

    def scores(k_ref, bias):
        k = k_ref[...]
        sq = [_segment_scores(seg_ref[...], k * q_ref[i:i + 1, :]) for i in range(n_q)]
        return [jnp.concatenate([x[mi * n_head:(mi + 1) * n_head] for x in sq], 0) * scale + bias for mi in range(2)]

    def absorb(s_maps, v_ref):
        v2 = v_ref[...].reshape(-1, v_ref.shape[-1]).astype(BF16)
        for mi, s in enumerate(s_maps):
            m = m_s[mi]
            mn = jnp.maximum(m, jnp.max(s, -1, keepdims=True))
            pr = jnp.exp(s - mn)
            alpha = jnp.exp(m - mn)
            m_s[mi] = mn
            l_s[mi] = alpha * l_s[mi] + jnp.sum(pr, -1, keepdims=True)
            a_s[mi] = alpha * a_s[mi] + _spread_dot(pr, own_ref[...], spread_ref[...], v2)

    @pl.when(p < n_step)
    def _():
        s_a, s_b = scores(ka_ref, bias_ref[0]), scores(kb_ref, bias_ref[1])
        absorb(s_a, va_ref)
        absorb(s_b, vb_ref)

    @pl.when(p == n_step)
    def _():
        absorb(scores(kn_ref, bias_new_ref[...]), vn_ref)
        o_ref[...] = _diff_finish(a_s[0], l_s[0], a_s[1], l_s[1], _lambda(lam_ref, lam_init), g_ref[...], lam_init)


def _diff_attention_sample(q, k_new, v_new, cache_k, cache_v, page_table, rel_bias, lam_vec, g, lam_init):
    db, n_q, width = q.shape
    n_head = width // (2 * D_HA)
    dv = cache_v.shape[-1]
    n_page = page_table.shape[1]
    past = n_page * PAGE
    pad_new = lambda a: jnp.pad(a, ((0, 0), (0, PAGE - n_q), (0, 0)))
    qpos = past + jnp.arange(n_q, dtype=jnp.int32)
    kpos = jnp.arange(past + PAGE, dtype=jnp.int32)
    bias = rel_bias[_t5_bucket(qpos[:, None] - kpos[None, :])]
    bias = jnp.where((kpos[None, :] <= qpos[:, None])[..., None], bias, NEG_BIG)
    bias = jnp.transpose(bias, (0, 2, 1)).reshape(n_q * n_head, n_page + 1, PAGE)
    bias = jnp.transpose(bias, (1, 0, 2))
    seg = _segment_matrix(2 * n_head, D_HA, [(i % 2) * n_head + i // 2 for i in range(2 * n_head)])
    rows = n_q * n_head
    spread, _, own = _key_head_matrices(PAGE, n_head, rows)
    assert n_page % 2 == 0
    n_step = n_page // 2
    page_of = lambda j: (lambda b, p, pt: (pt[b, jnp.minimum(2 * p, n_page - 2) + j], 0, 0))
    page4_of = lambda j: (lambda b, p, pt: (pt[b, jnp.minimum(2 * p, n_page - 2) + j], 0, 0, 0))
    const = lambda b, p, pt: (0, 0)
    grid_spec = pltpu.PrefetchScalarGridSpec(
        num_scalar_prefetch=1,
        grid=(db, n_step + 1),
        in_specs=[pl.BlockSpec((None, n_q, width), lambda b, p, pt: (b, 0, 0)),
                  pl.BlockSpec((None, PAGE, width), page_of(0)),
                  pl.BlockSpec((None, PAGE, n_head, dv), page4_of(0)),
                  pl.BlockSpec((None, PAGE, width), page_of(1)),
                  pl.BlockSpec((None, PAGE, n_head, dv), page4_of(1)),
                  pl.BlockSpec((None, PAGE, width), lambda b, p, pt: (b, 0, 0)),
                  pl.BlockSpec((None, PAGE, n_head, dv), lambda b, p, pt: (b, 0, 0, 0)),
                  pl.BlockSpec((None, 2, rows, PAGE), lambda b, p, pt: (jnp.minimum(p, n_step - 1), 0, 0, 0)),
                  pl.BlockSpec((rows, PAGE), const),
                  pl.BlockSpec(seg.shape, const),
                  pl.BlockSpec(own.shape, const),
                  pl.BlockSpec(spread.shape, const),
                  pl.BlockSpec((4, D_HA), const),
                  pl.BlockSpec((1, dv), const)],
        out_specs=pl.BlockSpec((None, rows, dv), lambda b, p, pt: (b, 0, 0)),
        scratch_shapes=[pltpu.VMEM((2, rows, 1), F32), pltpu.VMEM((2, rows, 1), F32),
                        pltpu.VMEM((2, rows, dv), F32)])
    out = pl.pallas_call(
        functools.partial(_diff_sample_kernel, n_q, n_step, lam_init),
        grid_spec=grid_spec,
        out_shape=jax.ShapeDtypeStruct((db, rows, dv), F32),
        compiler_params=_cparams("parallel", "arbitrary"),
        name="diff_attn_sample",
    )(page_table, q, cache_k, cache_v, cache_k, cache_v, pad_new(k_new),
      pad_new(v_new).reshape(db, PAGE, n_head, dv), bias[:n_page].reshape(n_step, 2, rows, PAGE), bias[n_page],
      seg, own, spread, lam_vec, g.reshape(1, -1))
    return out.reshape(db, n_q, width)


def _sb_sample_kernel(scale, pt_ref, q_ref, ka_ref, va_ref, kb_ref, vb_ref, kn_ref, vn_ref, mask_ref, own_ref,
                      spread_ref, collapse_ref, o_ref, run_s, a_s):
    p = pl.program_id(1)
    keys, _, dh = ka_ref.shape
    suffix = _suffix_matrix(keys)
    q = q_ref[...].astype(BF16)

    def front(k_ref, mask):
        k2 = k_ref[...].reshape(-1, dh).astype(BF16)
        z = _dot_exact_rhs(_dot_nt(q, k2) * own_ref[...], collapse_ref[...], parts=2) * scale
        ls, lk = _log_sigmoid_pair(z)
        if mask is not None:
            lk = lk * mask
        return ls, _dot_exact_rhs(lk, suffix, parts=2), jnp.sum(lk, -1, keepdims=True)

    def absorb(fronts, v_refs, mask):
        run = run_s[...]
        total = a_s[...]
        for (ls, after, mass), v_ref in zip(fronts, v_refs):
            a = jnp.exp(ls + after + run)
            if mask is not None:
                a = a * mask
            total = total + _spread_dot(a, own_ref[...], spread_ref[...], v_ref[...].reshape(-1, dh).astype(BF16))
            run = run + mass
        run_s[...] = run
        a_s[...] = total

    @pl.when(p == 0)
    def _():
        run_s[...] = jnp.zeros(run_s.shape, F32)
        a_s[...] = jnp.zeros(a_s.shape, F32)
        absorb([front(kn_ref, mask_ref[...])], [vn_ref], mask_ref[...])

    @pl.when(p > 0)
    def _():
        absorb([front(ka_ref, None), front(kb_ref, None)], [va_ref, vb_ref], None)

    @pl.when(p == pl.num_programs(1) - 1)
    def _():
        o_ref[...] = a_s[...]


def _sb_attention_sample(q, k_new, v_new, cache_k, cache_v, page_table):
    db, n_q, width = q.shape
    n_head, dh = cache_k.shape[2:]
    n_page = page_table.shape[1]
    pad_new = lambda a: jnp.pad(a, ((0, 0), (0, PAGE - n_q), (0, 0))).reshape(db, PAGE, n_head, dh)
    qi = jnp.repeat(jnp.arange(n_q, dtype=jnp.int32), n_head)[:, None]
    mask = (jnp.arange(PAGE, dtype=jnp.int32)[None, :] < qi).astype(F32)
    rows = n_q * n_head
    spread, collapse, own = _key_head_matrices(PAGE, n_head, rows)
    assert n_page % 2 == 0
    page_of = lambda j: (lambda b, p, pt: (pt[b, n_page - 2 * jnp.maximum(p, 1) + j], 0, 0, 0))
    const = lambda b, p, pt: (0, 0)
    grid_spec = pltpu.PrefetchScalarGridSpec(
        num_scalar_prefetch=1,
        grid=(db, n_page // 2 + 1),
        in_specs=[pl.BlockSpec((None, rows, dh), lambda b, p, pt: (b, 0, 0)),
                  pl.BlockSpec((None, PAGE, n_head, dh), page_of(1)),
                  pl.BlockSpec((None, PAGE, n_head, dh), page_of(1)),
                  pl.BlockSpec((None, PAGE, n_head, dh), page_of(0)),
                  pl.BlockSpec((None, PAGE, n_head, dh), page_of(0)),
                  pl.BlockSpec((None, PAGE, n_head, dh), lambda b, p, pt: (b, 0, 0, 0)),
                  pl.BlockSpec((None, PAGE, n_head, dh), lambda b, p, pt: (b, 0, 0, 0)),
                  pl.BlockSpec(mask.shape, const),
                  pl.BlockSpec(own.shape, const),
                  pl.BlockSpec(spread.shape, const),
                  pl.BlockSpec(collapse.shape, const)],
        out_specs=pl.BlockSpec((None, rows, dh), lambda b, p, pt: (b, 0, 0)),
        scratch_shapes=[pltpu.VMEM((rows, 1), F32), pltpu.VMEM((rows, dh), F32)])
    out = pl.pallas_call(
        functools.partial(_sb_sample_kernel, dh ** -0.5),
        grid_spec=grid_spec,
        out_shape=jax.ShapeDtypeStruct((db, rows, dh), F32),
        compiler_params=_cparams("parallel", "arbitrary"),
        name="sb_attn_sample",
    )(page_table, q.reshape(db, rows, dh), cache_k, cache_v, cache_k, cache_v, pad_new(k_new), pad_new(v_new), mask,
      own, spread, collapse)
    return out.reshape(db, n_q, width)


def _top_k_rows(arrays, k):
    n, lanes = arrays[0].shape
    row = lax.broadcasted_iota(jnp.int32, (n, lanes), 0).astype(F32)
    slot = lax.broadcasted_iota(jnp.int32, (k, lanes), 0)

    def body(i, carry):
        out = []
        for s, vals, ids in carry:
            m = jnp.max(s, 0, keepdims=True)
            first = jnp.min(jnp.where(s == m, row, float(n)), 0, keepdims=True)
            out.append((jnp.where(row == first, -jnp.inf, s), jnp.where(slot == i, m, vals),
                        jnp.where(slot == i, first, ids)))
        return tuple(out)

    zero = jnp.zeros((k, lanes), F32)
    done = lax.fori_loop(0, k, body, tuple((s, zero, zero) for s in arrays))
    return [(vals, ids) for _, vals, ids in done]


def _staircase_rows(a, b, fn, fill):
    k = a.shape[0]
    rows = [fn(a[i:i + 1, :], b[0:k // (i + 1), :]) for i in range(k)]
    n = sum(r.shape[0] for r in rows)
    rows.append(jnp.full((-n % 8, a.shape[1]), fill, F32))
    return jnp.concatenate(rows, 0)


def _peer_route_kernel(wqt_ref, keys_ref, x_ref, idx_ref, g_ref):
    nk = keys_ref.shape[2]
    qt = _dot_nt(wqt_ref[...], x_ref[...].astype(BF16)).astype(BF16)
    half = keys_ref.shape[3]
    for h in range(keys_ref.shape[0]):
        scores = [_dot(keys_ref[h, p], qt[(h * 2 + p) * half:(h * 2 + p + 1) * half, :]) for p in range(2)]
        (s1, i1), (s2, i2) = _top_k_rows(scores, PEER_TOPK)
        cand = _staircase_rows(s1, s2, lambda a, b: a + b, -jnp.inf)
        expert = _staircase_rows(i1, i2, lambda a, b: a * nk + b, 0.0)
        (sc, ci), = _top_k_rows([cand], PEER_TOPK)
        crow = lax.broadcasted_iota(jnp.int32, cand.shape, 0).astype(F32)
        picked = [jnp.sum(jnp.where(crow == ci[j:j + 1, :], expert, 0.0), 0, keepdims=True) for j in range(PEER_TOPK)]
        e = jnp.exp(sc - jnp.max(sc, 0, keepdims=True))
        idx_ref[h] = jnp.concatenate(picked, 0).astype(jnp.int32)
        g_ref[h] = e / jnp.sum(e, 0, keepdims=True)


def _peer_route(x, wqt_bf16, subkeys_bf16):
    t, d = x.shape
    heads = subkeys_bf16.shape[0]
    tt = 128
    return pl.pallas_call(
        _peer_route_kernel,
        grid=(t // tt,),
        in_specs=[pl.BlockSpec(wqt_bf16.shape, lambda i: (0, 0)),
                  pl.BlockSpec(subkeys_bf16.shape, lambda i: (0, 0, 0, 0)),
                  pl.BlockSpec((tt, d), lambda i: (i, 0))],
        out_specs=[pl.BlockSpec((heads, PEER_TOPK, tt), lambda i: (0, 0, i)),
                   pl.BlockSpec((heads, PEER_TOPK, tt), lambda i: (0, 0, i))],
        out_shape=[jax.ShapeDtypeStruct((heads, PEER_TOPK, t), jnp.int32),
                   jax.ShapeDtypeStruct((heads, PEER_TOPK, t), F32)],
        compiler_params=_cparams("parallel"),
        name="peer_route",
    )(wqt_bf16, subkeys_bf16, x)


PEER_TOKEN_CHUNK = 1024
PEER_EXPERT_TILE = 256
PEER_PAIR_BLOCK = 2048


def _gelu(x):
    return 0.5 * x * (1.0 + lax.erf(x * (2.0 ** -0.5)))


def _peer_expert_kernel(rpt, meta_ref, trow_ref, erow_ref, gate_ref, x_ref, u_ref, v_ref, o_ref, stage, act):
    i = pl.program_id(0)
    first, lo, hi = meta_ref[2, i], meta_ref[3, i], meta_ref[4, i]

    @pl.when(first == 1)
    def _():
        o_ref[...] = jnp.zeros(o_ref.shape, F32)

    @pl.when(i == 0)
    def _():
        stage[...] = jnp.zeros(stage.shape, F32)

    @pl.when(hi > lo)
    def _():
        _peer_segment(rpt, trow_ref, erow_ref, gate_ref, x_ref, u_ref, v_ref, o_ref, stage, act, lo, hi)


def _peer_segment(rpt, trow_ref, erow_ref, gate_ref, x_ref, u_ref, v_ref, o_ref, stage, act, lo, hi):
    def rows(ref, j):
        return pl.ds(pl.multiple_of(ref[0, j], rpt), rpt)

    def sweep(one, group, width):
        g_lo, g_hi = (lo + width - 1) // width, hi // width
        head_end = jnp.minimum(g_lo * width, hi)
        tail_start = jnp.maximum(g_hi * width, head_end)
        lax.fori_loop(lo, head_end, lambda j, c: (one(j), c)[1], 0)
        lax.fori_loop(g_lo, g_hi, lambda s, c: (group(s), c)[1], 0)
        lax.fori_loop(tail_start, hi, lambda j, c: (one(j), c)[1], 0)

    def dot_pair(j, row):
        pr = x_ref[rows(trow_ref, j), :] * u_ref[rows(erow_ref, j), :]
        acc = pr[0:8]
        for c in range(1, rpt // 8):
            acc = acc + pr[c * 8:(c + 1) * 8]
        stage[pl.ds(row, 8), :] = acc

    def dot_group(s):
        base = pl.multiple_of(s * 128, 128)
        for r in range(16):
            dot_pair(s * 16 + r, base + r * 8)

    sweep(lambda j: dot_pair(j, pl.multiple_of(j * 8, 8)), dot_group, 16)

    rr = lax.broadcasted_iota(jnp.int32, (128, 128), 0)
    cc = lax.broadcasted_iota(jnp.int32, (128, 128), 1)
    ones = jnp.ones((128, 128), BF16)

    def weights(g):
        base = pl.multiple_of(g * 1024, 1024)
        part = stage[pl.ds(base, 128, stride=8), :]
        for s in range(1, 8):
            part = part + stage[pl.ds(base + s, 128, stride=8), :]
        a = _gelu(_dot_exact_rhs(part, ones, parts=2))
        gate = _dot_exact_rhs(jnp.where(rr == cc, gate_ref[pl.ds(g, 1), :], 0.0), ones, parts=2)
        act[pl.ds(pl.multiple_of(g * 16, 16), 16)] = (a * gate).reshape(16, 8, 128)

    lax.fori_loop(lo // 256, (hi + 255) // 256, lambda t, c: (weights(2 * t), weights(2 * t + 1), c)[2], 0)

    def mix_one(j):
        out = rows(trow_ref, j)
        o_ref[out, :] = o_ref[out, :] + v_ref[rows(erow_ref, j), :] * act[j // 8, pl.ds(j % 8, 1), :]

    def mix_group(s):
        toks, news = [], []
        acc = None
        for r in range(8):
            j = s * 8 + r
            tok = trow_ref[0, j]
            c = v_ref[rows(erow_ref, j), :] * act[s, r:r + 1, :]
            acc = c if r == 0 else jnp.where(tok == toks[-1], acc, 0.0) + c
            toks.append(tok)
            news.append(o_ref[pl.ds(pl.multiple_of(tok, rpt), rpt), :] + acc)
        for tok, new in zip(toks, news):
            o_ref[pl.ds(pl.multiple_of(tok, rpt), rpt), :] = new

    sweep(mix_one, mix_group, 8)


def _peer_plan(idx, gates, t_pad, rpt):
    heads, topk, t = idx.shape
    pb = PEER_PAIR_BLOCK
    e_bits = PEER_EXPERT_TILE.bit_length() - 1
    c_bits = e_bits + PEER_TOKEN_CHUNK.bit_length() - 1
    n_tile = (PEER_NK * PEER_NK) // PEER_EXPERT_TILE
    n_cell = (t_pad // PEER_TOKEN_CHUNK) * n_tile
    n_chunk = t_pad // PEER_TOKEN_CHUNK
    per_chunk = PEER_TOKEN_CHUNK * heads * topk
    assert per_chunk % pb == 0 and ((n_cell + 1) << c_bits) < 2 ** 31
    n_blk = n_chunk * per_chunk // pb
    e = jnp.transpose(idx, (2, 0, 1)).reshape(-1)
    g = jnp.transpose(gates, (2, 0, 1)).reshape(-1)
    tok = jnp.repeat(jnp.arange(t, dtype=jnp.int32), heads * topk)
    cell = (tok // PEER_TOKEN_CHUNK) * n_tile + e // PEER_EXPERT_TILE
    key = (cell << c_bits) | ((tok % PEER_TOKEN_CHUNK) << e_bits) | (e % PEER_EXPERT_TILE)
    fill = n_chunk * per_chunk - key.shape[0]
    key = jnp.pad(key, (0, fill), constant_values=n_cell << c_bits).reshape(n_chunk, per_chunk)
    key_s, g_s = lax.sort((key, jnp.pad(g, (0, fill)).reshape(n_chunk, per_chunk)), dimension=1, num_keys=1,
                          is_stable=False)
    below = ((jnp.arange(n_chunk, dtype=jnp.int32)[:, None] * n_tile
              + jnp.arange(n_tile + 1, dtype=jnp.int32)[None, :]) << c_bits)
    n_below = jnp.sum((key_s[:, None, :] < below[:, :, None]).astype(jnp.int32), -1)
    n_below = n_below + jnp.arange(n_chunk, dtype=jnp.int32)[:, None] * per_chunk
    bounds = jnp.concatenate([n_below[:, :n_tile].reshape(-1), n_below[-1:, n_tile]])
    key_s, g_s = key_s.reshape(-1), g_s.reshape(-1)
    first_blk = bounds[:-1] // pb
    n_seg_cell = jnp.where(bounds[1:] > bounds[:-1], (bounds[1:] - 1) // pb - first_blk + 1, 0)
    seg_end = jnp.cumsum(n_seg_cell)
    step = jnp.arange(n_blk + n_cell, dtype=jnp.int32)
    last = seg_end[-1] - 1
    step_c = jnp.minimum(step, last)
    q = jnp.minimum(jnp.searchsorted(seg_end, step_c, side='right', method='compare_all'), n_cell - 1).astype(jnp.int32)
    blk = first_blk[q] + step_c - (seg_end - n_seg_cell)[q]
    lo = jnp.where(step <= last, jnp.clip(bounds[q] - blk * pb, 0, pb), 0)
    hi = jnp.where(step <= last, jnp.clip(bounds[q + 1] - blk * pb, 0, pb), 0)
    chunk = q // n_tile
    first = jnp.concatenate([jnp.ones((1,), jnp.int32), (chunk[1:] != chunk[:-1]).astype(jnp.int32)])
    meta = jnp.stack([chunk, q % n_tile, first, lo, hi, blk]).astype(jnp.int32)
    trow = ((key_s >> e_bits) & (PEER_TOKEN_CHUNK - 1)) * rpt
    erow = (key_s & (PEER_EXPERT_TILE - 1)) * rpt
    return meta, trow.reshape(n_blk, 1, pb), erow.reshape(n_blk, 1, pb), g_s.reshape(n_blk, pb // 128, 128)


def _peer_experts(x, idx, gates, u, v):
    t, d = x.shape
    t_pad = -(-t // PEER_TOKEN_CHUNK) * PEER_TOKEN_CHUNK
    rpt = d // 128
    meta, trow, erow, gate = _peer_plan(idx, gates, t_pad, rpt)
    rows = lambda a: a.reshape(a.shape[0] * rpt, 128)
    pairs = pl.BlockSpec((None, 1, PEER_PAIR_BLOCK), lambda i, m: (m[5, i], 0, 0), memory_space=pltpu.SMEM)
    tokens = pl.BlockSpec((PEER_TOKEN_CHUNK * rpt, 128), lambda i, m: (m[0, i], 0))
    experts = pl.BlockSpec((PEER_EXPERT_TILE * rpt, 128), lambda i, m: (m[1, i], 0))
    grid_spec = pltpu.PrefetchScalarGridSpec(
        num_scalar_prefetch=1,
        grid=(meta.shape[1],),
        in_specs=[pairs, pairs,
                  pl.BlockSpec((None, PEER_PAIR_BLOCK // 128, 128), lambda i, m: (m[5, i], 0, 0)),
                  tokens, experts, experts],
        out_specs=tokens,
        scratch_shapes=[pltpu.VMEM((PEER_PAIR_BLOCK * 8, 128), F32),
                        pltpu.VMEM((PEER_PAIR_BLOCK // 8, 8, 128), F32)])
    out = pl.pallas_call(
        functools.partial(_peer_expert_kernel, rpt),
        grid_spec=grid_spec,
        out_shape=jax.ShapeDtypeStruct((t_pad * rpt, 128), F32),
        compiler_params=pltpu.CompilerParams(dimension_semantics=("arbitrary",),
                                             vmem_limit_bytes=56 * 1024 * 1024),
        name="peer_experts",
    )(meta, trow, erow, gate, rows(jnp.pad(x, ((0, t_pad - t), (0, 0)))), rows(u), rows(v))
    return out.reshape(t_pad, d)[:t]


def _peer_layer(x, wq, subkeys, u, v, g, b, alpha):
    idx, gates = _peer_route(x, wq.T.astype(BF16), subkeys.astype(BF16))
    return _res_ln(x, _peer_experts(x, idx, gates, u, v), g, b, alpha)


def _pad_cols(w, mult):
    return jnp.pad(w, ((0, 0), (0, -w.shape[1] % mult)))


def kernel(x_prompt, x_sample, cache_diff_k, cache_diff_v, state_mlstm_C, state_mlstm_n, state_mlstm_m, state_mlstm_conv, cache_sb_k, cache_sb_v, page_table, rel_bias, w_in_even, b_ig, b_fg, lam_q1, lam_k1, lam_q2, lam_k2, diff_norm_g, conv_w, conv_b, w_q_mlstm, w_k_mlstm, mlstm_norm_g, mlstm_skip, w_out_even, w_qkv_odd, w_out_odd, ln_g, ln_b, peer_wq, peer_subkeys, peer_u, peer_v):
    nb, seq, d = x_prompt.shape
    db, n_q, _ = x_sample.shape
    tp, ts = nb * seq, db * n_q
    n_pool = cache_diff_k.shape[1]
    depth = ln_g.shape[0]
    alpha = (2.0 * depth) ** 0.25
    h_a, h_b, h_c = d // 256, w_q_mlstm.shape[1], d // 128
    w_a, w_b = h_a * 2 * D_HA, d // 2
    assert depth == 2 and w_in_even.shape[0] == 1 and w_qkv_odd.shape[0] == 1 and w_a == w_b
    lam_init = 0.8 - 0.6 * math.exp(-0.3 * 0)
    x = jnp.concatenate([x_prompt.reshape(tp, d), x_sample.reshape(ts, d)])

    n_main = 3 * w_a + 3 * w_b
    proj = _matmul(x, _pad_cols(w_in_even[0], 896).astype(BF16), 896)
    qa, ka, va, u_b = (proj[:, i * w_a:(i + 1) * w_a] for i in range(4))
    gates = proj[:, n_main:n_main + 2 * h_b]
    lam_vec = jnp.stack([lam_q1[0], lam_k1[0], lam_q2[0], lam_k2[0]])
    ya_p = _diff_attention_prompt(proj, nb, seq, h_a, rel_bias, lam_vec, diff_norm_g[0], lam_init, 256)
    smp = lambda a: a[tp:].reshape(db, n_q, -1)
    ya_s = _diff_attention_sample(smp(qa), smp(ka), smp(va), cache_diff_k[0].reshape(n_pool, PAGE, w_a),
                                  cache_diff_v[0], page_table, rel_bias, lam_vec,
                                  diff_norm_g[0], lam_init)
    mp = {'b_ig': b_ig[0], 'b_fg': b_fg[0], 'conv_w': conv_w[0], 'conv_b': conv_b[0], 'w_q_mlstm': w_q_mlstm[0],
          'w_k_mlstm': w_k_mlstm[0], 'mlstm_norm_g': mlstm_norm_g[0], 'mlstm_skip': mlstm_skip[0]}
    lane0 = 3 * w_a // 256
    yb_p, c_p, n_p, m_p = _mlstm(proj, gates[:tp], nb, seq, MLSTM_CHUNK, h_b, lane0,
                                 jnp.zeros((nb, 8, w_b), F32), jnp.zeros((nb, h_b, 256, 256), F32),
                                 jnp.zeros((nb, h_b, 256), F32), jnp.zeros((nb, h_b), F32), mp)
    s_pad = 8
    pad_t = lambda a, val=0.0: jnp.pad(a, ((0, 0), (0, s_pad - n_q), (0, 0)), constant_values=val)
    gate_s = smp(gates)
    gate_s = jnp.concatenate([pad_t(gate_s[..., :h_b], NEG_BIG), pad_t(gate_s[..., h_b:], -NEG_BIG)], -1)
    conv0 = jnp.pad(state_mlstm_conv[0], ((0, 0), (8 - (CONV_W - 1), 0), (0, 0)))
    yb_s, c_s, n_s, m_s = _mlstm(pad_t(smp(proj)).reshape(db * s_pad, -1), gate_s.reshape(db * s_pad, -1), db, s_pad,
                                 s_pad, h_b, lane0, conv0, state_mlstm_C[0], state_mlstm_n[0], state_mlstm_m[0], mp)
    yb_s = yb_s.reshape(db, s_pad, w_b)[:, :n_q].reshape(ts, w_b)
    y = jnp.concatenate([jnp.concatenate([ya_p, yb_p], -1), jnp.concatenate([ya_s.reshape(ts, w_a), yb_s], -1)])
    x = _matmul_res_ln(y, w_out_even[0].astype(BF16), x, ln_g[0, 0], ln_b[0, 0], alpha)
    x = _peer_layer(x, peer_wq[0], peer_subkeys[0], peer_u[0], peer_v[0], ln_g[0, 1], ln_b[0, 1], alpha)

    qkv = _matmul(x, w_qkv_odd[0].astype(BF16), 1024)
    y_p = _sb_attention_prompt(qkv, nb, seq, h_c, 128, 256)
    q_s, k_s, v_s = (smp(qkv[:, i * d:(i + 1) * d]) for i in range(3))
    y_s = _sb_attention_sample(q_s, k_s, v_s, cache_sb_k[0], cache_sb_v[0], page_table)
    y = jnp.concatenate([y_p, y_s.reshape(ts, d)])
    x = _matmul_res_ln(y, w_out_odd[0].astype(BF16), x, ln_g[1, 0], ln_b[1, 0], alpha)
    x = _peer_layer(x, peer_wq[1], peer_subkeys[1], peer_u[1], peer_v[1], ln_g[1, 1], ln_b[1, 1], alpha)

    prm = lambda a, *s: a[:tp].reshape((1, nb, seq) + s)
    u_p = proj[:tp, 3 * w_a:3 * w_a + w_b].reshape(nb, seq, w_b)
    u_s = smp(u_b)
    sbk, sbv = qkv[:, d:2 * d], qkv[:, 2 * d:]
    return (x[:tp].reshape(nb, seq, d), x[tp:].reshape(db, n_q, d),
            prm(ka, h_a, 2, D_HA), prm(va, h_a, 2 * D_HA),
            smp(ka).reshape(1, db, n_q, h_a, 2, D_HA), smp(va).reshape(1, db, n_q, h_a, 2 * D_HA),
            c_p[None], n_p[None], m_p[None], u_p[None, :, seq - (CONV_W - 1):],
            c_s[None], n_s[None], m_s[None], u_s[None, :, n_q - (CONV_W - 1):],
            prm(sbk, h_c, 128), prm(sbv, h_c, 128),
            smp(sbk).reshape(1, db, n_q, h_c, 128), smp(sbv).reshape(1, db, n_q, h_c, 128))
```
